```python
import jax, jax.numpy as jnp
from jax import lax
import numpy as np

D_MODEL = 1024
BATCH = 8
SEQ = 2048
DEPTH = 1

CHUNK = 64
D_MIX = D_MODEL
D_POOL = D_MIX // 2
POOL_WINDOWS = (2, 4, 8, 16)
N_POOL_GROUPS = len(POOL_WINDOWS)
POOL_GROUP = D_POOL // N_POOL_GROUPS
D_RWKV = D_MIX - D_POOL
RWKV_HEAD = 64
N_RWKV_HEADS = D_RWKV // RWKV_HEAD
DECAY_LORA = 64
AAA_LORA = 64
D_SHIFT = 3 * D_RWKV + DECAY_LORA + AAA_LORA
D_IN = 2 * D_POOL + D_SHIFT + D_RWKV
NORM_EPS = 1e-6
GN_EPS = 64e-5

kernel_name = "hybrid_pool_rwkv7_block"


def rmsnorm(x, gain):
    x32 = x.astype(jnp.float32)
    inv = lax.rsqrt(jnp.mean(x32 * x32, axis=-1, keepdims=True) + NORM_EPS)
    return (x32 * inv * gain.astype(jnp.float32)).astype(x.dtype)


def multi_scale_pool(u, pool_w, pool_scale):
    B, S, _ = u.shape
    u32 = u.astype(jnp.float32).reshape(B, S, N_POOL_GROUPS, POOL_GROUP)
    cs = jnp.cumsum(u32, axis=1)
    cs0 = jnp.concatenate([jnp.zeros_like(cs[:, :1]), cs], axis=1)
    pos = jnp.arange(S)
    outs = []
    for g, w in enumerate(POOL_WINDOWS):
        hi = cs[:, :, g]
        lo = jnp.concatenate([jnp.zeros((B, w - 1, POOL_GROUP), jnp.float32),
                              cs0[:, :S + 1 - w, g]], axis=1)
        cnt = jnp.minimum(pos + 1, w).astype(jnp.float32)[None, :, None]
        outs.append((hi - lo) / cnt - u32[:, :, g])
    pooled = jnp.stack(outs, axis=2)
    mixed = jnp.einsum('bsgc,gcd->bsgd', pooled, pool_w.astype(jnp.float32))
    return (mixed.reshape(B, S, D_POOL) * pool_scale.astype(jnp.float32)).astype(u.dtype)


def rwkv7_time_mix(feat, w0, w_up, a0, a_up, k_k, k_a, r_k, gn_gain, gn_bias):
    B, S, _ = feat.shape
    H, N = N_RWKV_HEADS, RWKV_HEAD
    f = feat.astype(jnp.float32)
    r = f[..., :D_RWKV]
    k = f[..., D_RWKV:2 * D_RWKV]
    v = f[..., 2 * D_RWKV:3 * D_RWKV]
    w_lr = f[..., 3 * D_RWKV:3 * D_RWKV + DECAY_LORA]
    a_lr = f[..., 3 * D_RWKV + DECAY_LORA:]
    ww = w0.astype(jnp.float32) + jnp.tanh(w_lr) @ w_up.astype(jnp.float32)
    decay = jnp.exp(-jnp.exp(-jax.nn.softplus(-ww) - 0.5))
    a = jax.nn.sigmoid(a0.astype(jnp.float32) + a_lr @ a_up.astype(jnp.float32))
    heads = lambda t: t.reshape(B, S, H, N)
    kk = heads(k * k_k.astype(jnp.float32))
    kk = kk / jnp.maximum(jnp.sqrt(jnp.sum(kk * kk, axis=-1, keepdims=True)), 1e-12)
    k = k * (1.0 + (a - 1.0) * k_a.astype(jnp.float32))
    r, k, v, decay, a = heads(r), heads(k), heads(v), heads(decay), heads(a)

    def step(state, inp):
        r_t, k_t, v_t, w_t, kk_t, a_t = inp
        sa = jnp.einsum('bhvk,bhk->bhv', state, -kk_t)
        state = (state * w_t[:, :, None, :]
                 + sa[..., None] * (kk_t * a_t)[:, :, None, :]
                 + v_t[..., None] * k_t[:, :, None, :])
        y_t = jnp.einsum('bhvk,bhk->bhv', state, r_t)
        return state, y_t

    tm = lambda t: jnp.moveaxis(t, 1, 0)
    state0 = jnp.zeros((B, H, N, N), jnp.float32)
    _, y = lax.scan(step, state0, (tm(r), tm(k), tm(v), tm(decay), tm(kk), tm(a)))
    y = jnp.moveaxis(y, 0, 1)
    mu = jnp.mean(y, axis=-1, keepdims=True)
    var = jnp.mean(jnp.square(y - mu), axis=-1, keepdims=True)
    y = (y - mu) * lax.rsqrt(var + GN_EPS)
    y = y * gn_gain.astype(jnp.float32).reshape(H, N) + gn_bias.astype(jnp.float32).reshape(H, N)
    bonus = jnp.sum(r * k * r_k.astype(jnp.float32), axis=-1, keepdims=True) * v
    return (y + bonus).reshape(B, S, D_RWKV)


def setup_inputs(seed: int = 0) -> dict:
    key = jax.random.key(seed)
    ks = jax.random.split(key, 20)
    f32 = jnp.float32
    nrm = lambda k, shape, s: jax.random.normal(k, shape, f32) * s
    x = jax.random.normal(ks[0], (BATCH, SEQ, D_MODEL), f32)
    norm_gain = 1.0 + nrm(ks[1], (DEPTH, D_MODEL), 0.05)
    w_in = nrm(ks[2], (DEPTH, D_MODEL, D_IN), D_MODEL ** -0.5)
    pool_w = nrm(ks[3], (DEPTH, N_POOL_GROUPS, POOL_GROUP, POOL_GROUP), POOL_GROUP ** -0.5)
    pool_scale = 1.0 + nrm(ks[4], (DEPTH, D_POOL), 0.05)
    shift_mu = jax.random.uniform(ks[5], (DEPTH, D_SHIFT), f32, 0.0, 1.0)
    w0 = jax.random.uniform(ks[6], (DEPTH, D_RWKV), f32, -4.0, 2.0)
    w_up = nrm(ks[7], (DEPTH, DECAY_LORA, D_RWKV), 0.1)
    a0 = nrm(ks[8], (DEPTH, D_RWKV), 0.1)
    a_up = nrm(ks[9], (DEPTH, AAA_LORA, D_RWKV), 0.5 * AAA_LORA ** -0.5)
    k_k = 0.85 + nrm(ks[10], (DEPTH, D_RWKV), 0.05)
    k_a = 1.0 + nrm(ks[11], (DEPTH, D_RWKV), 0.05)
    r_k = nrm(ks[12], (DEPTH, N_RWKV_HEADS, RWKV_HEAD), 0.1)
    gn_gain = 1.0 + nrm(ks[13], (DEPTH, D_RWKV), 0.05)
    gn_bias = nrm(ks[14], (DEPTH, D_RWKV), 0.02)
    w_out = nrm(ks[15], (DEPTH, D_MIX, D_MODEL), D_MIX ** -0.5)
    final_gain = 1.0 + nrm(ks[16], (D_MODEL,), 0.05)
    return {"x": x, "norm_gain": norm_gain, "w_in": w_in, "pool_w": pool_w,
            "pool_scale": pool_scale, "shift_mu": shift_mu, "w0": w0, "w_up": w_up,
            "a0": a0, "a_up": a_up, "k_k": k_k, "k_a": k_a, "r_k": r_k,
            "gn_gain": gn_gain, "gn_bias": gn_bias, "w_out": w_out, "final_gain": final_gain}


def reference(x, norm_gain, w_in, pool_w, pool_scale, shift_mu, w0, w_up, a0, a_up,
              k_k, k_a, r_k, gn_gain, gn_bias, w_out, final_gain):
    for l in range(DEPTH):
        h = rmsnorm(x, norm_gain[l])
        z = h @ w_in[l]
        u_a = z[..., :D_POOL]
        g_a = z[..., D_POOL:2 * D_POOL]
        sh = z[..., 2 * D_POOL:2 * D_POOL + D_SHIFT]
        g_b = z[..., 2 * D_POOL + D_SHIFT:]
        sh_prev = jnp.pad(sh, ((0, 0), (1, 0), (0, 0)))[:, :-1]
        sh = sh + shift_mu[l] * (sh_prev - sh)
        y_a = multi_scale_pool(u_a, pool_w[l], pool_scale[l])
        y_b = rwkv7_time_mix(sh, w0[l], w_up[l], a0[l], a_up[l], k_k[l], k_a[l],
                             r_k[l], gn_gain[l], gn_bias[l]).astype(x.dtype)
        y = jnp.concatenate([y_a * jax.nn.silu(g_a), y_b * jax.nn.silu(g_b)], axis=-1)
        x = x + y @ w_out[l]
    return rmsnorm(x, final_gain)
```

```python
import functools
import math

import jax
import jax.numpy as jnp
from jax import lax
from jax.experimental import pallas as pl
from jax.experimental.pallas import tpu as pltpu

F32 = jnp.float32
BF16 = jnp.bfloat16

D_MODEL = 1024
D_POOL = 512
POOL_WINDOWS = (2, 4, 8, 16)
POOL_GROUP = 128
POOL_HALO = 16
D_RWKV = 512
N_HEADS = 8
HEAD = 64
LORA = 64
D_SHIFT = 3 * D_RWKV + 2 * LORA
D_IN = 2 * D_POOL + D_SHIFT + D_RWKV
NORM_EPS = 1e-6
GN_EPS = 64e-5
CHUNK = 64
TILE = 256
DECAY_SCALE = math.exp(-0.5)
VMEM_LIMIT_BYTES = 48 * 1024 * 1024


def _mm(a, b):
    return jnp.dot(a.astype(BF16), b.astype(BF16), preferred_element_type=F32)


def _mm_nt(a, b):
    return lax.dot_general(a.astype(BF16), b.astype(BF16), (((1,), (1,)), ((), ())),
                           preferred_element_type=F32)


def _mm_tn(a, b):
    return lax.dot_general(a.astype(BF16), b.astype(BF16), (((0,), (0,)), ((), ())),
                           preferred_element_type=F32)


def _split(a, parts):
    out = []
    rem = a
    for _ in range(parts):
        p = rem.astype(BF16)
        out.append(p)
        rem = rem - p.astype(F32)
    return out


def _mm_exact_rhs(a, b_bf16, parts):
    acc = None
    for p in _split(a, parts):
        t = jnp.dot(p, b_bf16, preferred_element_type=F32)
        acc = t if acc is None else acc + t
    return acc


def _mm_exact_lhs(a_bf16, b, parts):
    acc = None
    for p in _split(b, parts):
        t = jnp.dot(a_bf16, p, preferred_element_type=F32)
        acc = t if acc is None else acc + t
    return acc


def _mm_hi(a, b):
    a_hi, a_lo = _split(a, 2)
    b_hi, b_lo = _split(b, 2)
    d = functools.partial(jnp.dot, preferred_element_type=F32)
    return d(a_hi, b_hi) + (d(a_hi, b_lo) + d(a_lo, b_hi))


def _sigmoid(t):
    return 1.0 / (1.0 + jnp.exp(-t))


def _silu(t):
    return t * _sigmoid(t)


def _block_kernel(x_ref, ng_ref, win_ref, poolw_ref, pscale_ref, mu_ref, w0_ref, lora_ref,
                  a0_ref, kk_ref, ka_ref, rk_ref, gng_ref, gnb_ref, wout_ref, fg_ref,
                  headsum_ref, cumtri_ref, chunkones_ref,
                  o_ref,
                  shift_carry, pool_carry, state,
                  at_s, rt_s, bt_s, kt_s, bh_s, kh_s, v_s, ge_s, y_s):
    j = pl.program_id(1)

    @pl.when(j == 0)
    def _():
        shift_carry[...] = jnp.zeros_like(shift_carry)
        pool_carry[...] = jnp.zeros_like(pool_carry)
        state[...] = jnp.zeros_like(state)

    x = x_ref[0]
    inv = lax.rsqrt(jnp.mean(x * x, axis=-1, keepdims=True) + NORM_EPS)
    h = (x * inv * ng_ref[...]).astype(BF16)

    row = lax.broadcasted_iota(jnp.int32, (TILE, 1), 0)

    z_pool = jnp.dot(h, win_ref[:, 0:2 * D_POOL], preferred_element_type=F32)
    u = z_pool[:, :D_POOL]
    g_a = z_pool[:, D_POOL:]
    ext = jnp.concatenate([pool_carry[...], u], axis=0)
    pool_carry[...] = u[TILE - POOL_HALO:, :]
    pos = (j * TILE + row).astype(F32)
    pooled = []
    s = ext
    for g, w in enumerate(POOL_WINDOWS):
        lo = g * POOL_GROUP
        s = s[:, (POOL_GROUP if g > 0 else 0):]
        s = s + pltpu.roll(s, w // 2, axis=0)
        cnt = jnp.minimum(pos + 1.0, float(w))
        pooled.append(s[POOL_HALO:, :POOL_GROUP] / cnt - u[:, lo:lo + POOL_GROUP])
    mixed = [_mm(pooled[g], poolw_ref[g]) for g in range(len(POOL_WINDOWS))]
    y_a = jnp.concatenate(mixed, axis=1) * pscale_ref[...] * _silu(g_a)

    z_sh = jnp.dot(h, win_ref[:, 2 * D_POOL:2 * D_POOL + D_SHIFT], preferred_element_type=F32)
    prev = pltpu.roll(z_sh, 1, axis=0)
    prev = jnp.where(row == 0, shift_carry[...], prev)
    shift_carry[...] = z_sh[TILE - 1:TILE, :]
    sh = z_sh + mu_ref[...] * (prev - z_sh)

    r = sh[:, 0:D_RWKV]
    k = sh[:, D_RWKV:2 * D_RWKV]
    v = sh[:, 2 * D_RWKV:3 * D_RWKV]
    lr = sh[:, 3 * D_RWKV:]
    lane = lax.broadcasted_iota(jnp.int32, (1, 2 * LORA), 1)
    lr = jnp.where(lane < LORA, jnp.tanh(lr), lr)
    lora = _mm_hi(lr, lora_ref[...])
    ww = w0_ref[...] + lora[:, :D_RWKV]
    a = _sigmoid(a0_ref[...] + lora[:, D_RWKV:])
    logw = -DECAY_SCALE * _sigmoid(ww)

    headsum = headsum_ref[...]
    kkv = k * kk_ref[...]
    ss = _mm_exact_rhs(kkv * kkv, headsum, 2)
    kkn = kkv / jnp.maximum(jnp.sqrt(ss), 1e-12)
    k2 = k * (1.0 + (a - 1.0) * ka_ref[...])
    bonus = _mm_exact_rhs(r * k2 * rk_ref[...], headsum, 2) * v

    gcum = _mm_exact_lhs(cumtri_ref[...], logw, 3)
    gend = _mm_exact_lhs(chunkones_ref[...], logw, 3)
    e_inv = jnp.exp(-gcum)
    e_end = jnp.exp(gend - gcum)
    kka = kkn * a
    at_s[...] = -kkn * jnp.exp(gcum - logw)
    rt_s[...] = r * jnp.exp(gcum)
    bt_s[...] = kka * e_inv
    kt_s[...] = k2 * e_inv
    bh_s[...] = kka * e_end
    kh_s[...] = k2 * e_end
    v_s[...] = v
    ge_s[...] = jnp.exp(gend)

    ri = lax.broadcasted_iota(jnp.int32, (CHUNK, CHUNK), 0)
    ci = lax.broadcasted_iota(jnp.int32, (CHUNK, CHUNK), 1)
    tri_s = ri > ci
    tri_i = ri >= ci
    eye = ri == ci
    n_doubling = int(math.log2(CHUNK))

    def chunk_body(c, carry):
        rows = pl.ds(pl.multiple_of(c * CHUNK, CHUNK), CHUNK)
        for hd in range(N_HEADS):
            cols = slice(hd * HEAD, (hd + 1) * HEAD)
            a_ = at_s[rows, cols]
            r_ = rt_s[rows, cols]
            b_ = bt_s[rows, cols]
            k_ = kt_s[rows, cols]
            v_ = v_s[rows, cols]
            aall = _mm_nt(jnp.concatenate([a_, r_], axis=0), jnp.concatenate([b_, k_], axis=0))
            a_ab = jnp.where(tri_s, aall[:CHUNK, :CHUNK], 0.0)
            a_ak = jnp.where(tri_s, aall[:CHUNK, CHUNK:], 0.0)
            a_rb = jnp.where(tri_i, aall[CHUNK:, :CHUNK], 0.0)
            a_rk = jnp.where(tri_i, aall[CHUNK:, CHUNK:], 0.0)
            av = _mm(jnp.concatenate([a_ak, a_rk], axis=0), v_)
            z = jnp.concatenate([a_, av[:CHUNK]], axis=1)
            xp = a_ab
            for it in range(n_doubling):
                z = z + _mm(xp, z)
                if it + 1 < n_doubling:
                    xp = _mm(xp, xp)
            ry = jnp.concatenate([r_, av[CHUNK:]], axis=1) + _mm(a_rb, z)
            mg = _mm_tn(z, bh_s[rows, cols])
            m_t = mg[:HEAD] + jnp.where(eye, ge_s[rows, cols][0:1, :], 0.0)
            g0_t = mg[HEAD:] + _mm_tn(v_, kh_s[rows, cols])
            s_prev = state[hd]
            y_s[rows, cols] = _mm_nt(ry[:, :HEAD], s_prev) + ry[:, HEAD:]
            state[hd] = _mm(s_prev, m_t) + g0_t
        return carry

    lax.fori_loop(0, TILE // CHUNK, chunk_body, 0)

    y = y_s[...]
    mean = _mm_exact_rhs(y, headsum, 2) * (1.0 / HEAD)
    d = y - mean
    var = _mm_exact_rhs(d * d, headsum, 2) * (1.0 / HEAD)
    y = d * lax.rsqrt(var + GN_EPS) * gng_ref[...] + gnb_ref[...]
    g_b = jnp.dot(h, win_ref[:, 2 * D_POOL + D_SHIFT:], preferred_element_type=F32)
    y_b = (y + bonus) * _silu(g_b)

    yy = jnp.concatenate([y_a, y_b], axis=1).astype(BF16)
    xo = x + jnp.dot(yy, wout_ref[...], preferred_element_type=F32)
    inv_o = lax.rsqrt(jnp.mean(xo * xo, axis=-1, keepdims=True) + NORM_EPS)
    o_ref[0] = xo * inv_o * fg_ref[...]


def _full(shape):
    return pl.BlockSpec(shape, lambda b, j: (0,) * len(shape))


def kernel(x, norm_gain, w_in, pool_w, pool_scale, shift_mu, w0, w_up, a0, a_up, k_k, k_a, r_k,
           gn_gain, gn_bias, w_out, final_gain):
    batch, seq, d_model = x.shape
    assert norm_gain.shape[0] == 1 and d_model == D_MODEL and seq % TILE == 0
    assert w_in.shape == (1, D_MODEL, D_IN)

    lora_w = jnp.zeros((2 * LORA, 2 * D_RWKV), F32)
    lora_w = lora_w.at[:LORA, :D_RWKV].set(w_up[0]).at[LORA:, D_RWKV:].set(a_up[0])
    hid = jnp.arange(D_RWKV) // HEAD
    headsum = (hid[:, None] == hid[None, :]).astype(BF16)
    t = jnp.arange(TILE)
    same_chunk = (t[:, None] // CHUNK) == (t[None, :] // CHUNK)
    cumtri = (same_chunk & (t[:, None] >= t[None, :])).astype(BF16)
    chunkones = same_chunk.astype(BF16)
    row2 = lambda p: p.reshape(1, -1).astype(F32)

    operands = (
        x, row2(norm_gain[0]), w_in[0].astype(BF16), pool_w[0].astype(BF16), row2(pool_scale[0]),
        row2(shift_mu[0]), row2(w0[0]), lora_w, row2(a0[0]), row2(k_k[0]), row2(k_a[0]),
        row2(r_k[0]), row2(gn_gain[0]), row2(gn_bias[0]), w_out[0].astype(BF16), row2(final_gain),
        headsum, cumtri, chunkones,
    )
    in_specs = [pl.BlockSpec((1, TILE, D_MODEL), lambda b, j: (b, j, 0))]
    in_specs += [_full(op.shape) for op in operands[1:]]
    seq_buf = lambda: pltpu.VMEM((TILE, D_RWKV), F32)
    return pl.pallas_call(
        _block_kernel,
        grid=(batch, seq // TILE),
        in_specs=in_specs,
        out_specs=pl.BlockSpec((1, TILE, D_MODEL), lambda b, j: (b, j, 0)),
        out_shape=jax.ShapeDtypeStruct(x.shape, x.dtype),
        scratch_shapes=[
            pltpu.VMEM((1, D_SHIFT), F32),
            pltpu.VMEM((POOL_HALO, D_POOL), F32),
            pltpu.VMEM((N_HEADS, HEAD, HEAD), F32),
        ] + [seq_buf() for _ in range(9)],
        compiler_params=pltpu.CompilerParams(
            dimension_semantics=("arbitrary", "arbitrary"),
            vmem_limit_bytes=VMEM_LIMIT_BYTES),
        name="hybrid_pool_rwkv7_block",
    )(*operands)
```

```python
import functools
import math

import jax
import jax.numpy as jnp
from jax import lax
from jax.experimental import pallas as pl
from jax.experimental.pallas import tpu as pltpu

F32 = jnp.float32
BF16 = jnp.bfloat16

D_MODEL = 1024
D_POOL = 512
POOL_WINDOWS = (2, 4, 8, 16)
POOL_GROUP = 128
POOL_HALO = 16
D_RWKV = 512
N_HEADS = 8
HEAD = 64
PAIR = 2 * HEAD
LORA = 64
D_SHIFT = 3 * D_RWKV + 2 * LORA
D_IN = 2 * D_POOL + D_SHIFT + D_RWKV
NORM_EPS = 1e-6
GN_EPS = 64e-5
CHUNK = 64
TILE = 256
DECAY_SCALE = math.exp(-0.5)
VMEM_LIMIT_BYTES = 48 * 1024 * 1024


def _mm(a, b):
    return jnp.dot(a.astype(BF16), b.astype(BF16), preferred_element_type=F32)


def _mm_nt(a, b):
    return lax.dot_general(a.astype(BF16), b.astype(BF16), (((1,), (1,)), ((), ())),
                           preferred_element_type=F32)


def _mm_tn(a, b):
    return lax.dot_general(a.astype(BF16), b.astype(BF16), (((0,), (0,)), ((), ())),
                           preferred_element_type=F32)


def _split(a, parts):
    out = []
    rem = a
    for _ in range(parts):
        p = rem.astype(BF16)
        out.append(p)
        rem = rem - p.astype(F32)
    return out


def _mm_exact_rhs(a, b_bf16, parts):
    acc = None
    for p in _split(a, parts):
        t = jnp.dot(p, b_bf16, preferred_element_type=F32)
        acc = t if acc is None else acc + t
    return acc


def _mm_exact_lhs(a_bf16, b, parts):
    acc = None
    for p in _split(b, parts):
        t = jnp.dot(a_bf16, p, preferred_element_type=F32)
        acc = t if acc is None else acc + t
    return acc


def _mm_hi(a, b):
    a_hi, a_lo = _split(a, 2)
    b_hi, b_lo = _split(b, 2)
    d = functools.partial(jnp.dot, preferred_element_type=F32)
    return d(a_hi, b_hi) + (d(a_hi, b_lo) + d(a_lo, b_hi))


def _sigmoid(t):
    return 1.0 / (1.0 + jnp.exp(-t))


def _silu(t):
    return t * _sigmoid(t)


def _block_kernel(x_ref, ng_ref, win_ref, poolw_ref, pscale_ref, mu_ref, w0_ref, lora_ref,
                  a0_ref, kk_ref, ka_ref, rk_ref, gng_ref, gnb_ref, wout_ref, fg_ref,
                  headsum_ref, headswap_ref, cumtri_ref, chunkones_ref,
                  o_ref,
                  shift_carry, pool_carry, state,
                  at_s, rt_s, bt_s, kt_s, bh_s, kh_s, v_s, ge_s, y_s):
    j = pl.program_id(1)

    @pl.when(j == 0)
    def _():
        shift_carry[...] = jnp.zeros_like(shift_carry)
        pool_carry[...] = jnp.zeros_like(pool_carry)
        state[...] = jnp.zeros_like(state)

    x = x_ref[0]
    inv = lax.rsqrt(jnp.mean(x * x, axis=-1, keepdims=True) + NORM_EPS)
    h = (x * inv * ng_ref[...]).astype(BF16)

    row = lax.broadcasted_iota(jnp.int32, (TILE, 1), 0)

    z_pool = jnp.dot(h, win_ref[:, 0:2 * D_POOL], preferred_element_type=F32)
    u = z_pool[:, :D_POOL]
    g_a = z_pool[:, D_POOL:]
    ext = jnp.concatenate([pool_carry[...], u], axis=0)
    pool_carry[...] = u[TILE - POOL_HALO:, :]
    pos = (j * TILE + row).astype(F32)
    pooled = []
    s = ext
    for g, w in enumerate(POOL_WINDOWS):
        lo = g * POOL_GROUP
        s = s[:, (POOL_GROUP if g > 0 else 0):]
        s = s + pltpu.roll(s, w // 2, axis=0)
        cnt = jnp.minimum(pos + 1.0, float(w))
        pooled.append(s[POOL_HALO:, :POOL_GROUP] / cnt - u[:, lo:lo + POOL_GROUP])
    mixed = [_mm(pooled[g], poolw_ref[g]) for g in range(len(POOL_WINDOWS))]
    y_a = jnp.concatenate(mixed, axis=1) * pscale_ref[...] * _silu(g_a)

    z_sh = jnp.dot(h, win_ref[:, 2 * D_POOL:2 * D_POOL + D_SHIFT], preferred_element_type=F32)
    prev = pltpu.roll(z_sh, 1, axis=0)
    prev = jnp.where(row == 0, shift_carry[...], prev)
    shift_carry[...] = z_sh[TILE - 1:TILE, :]
    sh = z_sh + mu_ref[...] * (prev - z_sh)

    r = sh[:, 0:D_RWKV]
    k = sh[:, D_RWKV:2 * D_RWKV]
    v = sh[:, 2 * D_RWKV:3 * D_RWKV]
    lr = sh[:, 3 * D_RWKV:]
    lane = lax.broadcasted_iota(jnp.int32, (1, 2 * LORA), 1)
    lr = jnp.where(lane < LORA, jnp.tanh(lr), lr)
    lora = _mm_hi(lr, lora_ref[...])
    ww = w0_ref[...] + lora[:, :D_RWKV]
    a = _sigmoid(a0_ref[...] + lora[:, D_RWKV:])
    logw = -DECAY_SCALE * _sigmoid(ww)

    headsum = headsum_ref[...]
    kkv = k * kk_ref[...]
    ss = _mm_exact_rhs(kkv * kkv, headsum, 2)
    kkn = kkv / jnp.maximum(jnp.sqrt(ss), 1e-12)
    k2 = k * (1.0 + (a - 1.0) * ka_ref[...])
    bonus = _mm_exact_rhs(r * k2 * rk_ref[...], headswap_ref[...], 2) * v

    gcum = _mm_exact_lhs(cumtri_ref[...], logw, 3)
    gend = _mm_exact_lhs(chunkones_ref[...], logw, 3)
    e_inv = jnp.exp(-gcum)
    e_end = jnp.exp(gend - gcum)
    kka = kkn * a
    at_s[...] = -kkn * jnp.exp(gcum - logw)
    rt_s[...] = r * jnp.exp(gcum)
    bt_s[...] = kka * e_inv
    kt_s[...] = k2 * e_inv
    bh_s[...] = kka * e_end
    kh_s[...] = k2 * e_end
    v_s[...] = v
    ge_s[...] = jnp.exp(gend)

    ri = lax.broadcasted_iota(jnp.int32, (CHUNK, PAIR), 0)
    ci = lax.broadcasted_iota(jnp.int32, (CHUNK, PAIR), 1)
    cm = ci % CHUNK
    left = ci < CHUNK
    strict_l = (ri > cm) & left
    incl_l = (ri >= cm) & left
    tri_r = jnp.concatenate([(ri > cm) & ~left, (ri >= cm) & ~left], axis=0)
    lane_half = lax.broadcasted_iota(jnp.int32, (1, PAIR), 1) // HEAD
    key_mask = [lane_half == e for e in range(2)]
    eye_key = [ci == ri + e * HEAD for e in range(2)]
    zeros_blk = jnp.zeros((HEAD, PAIR), F32)
    n_doubling = int(math.log2(CHUNK))
    heads = [(p, e) for p in range(N_HEADS // 2) for e in range(2)]

    def chunk_body(c, carry):
        rows = pl.ds(pl.multiple_of(c * CHUNK, CHUNK), CHUNK)
        blk = lambda ref, p: ref[rows, p * PAIR:(p + 1) * PAIR]
        aall, z, ry, vm = [], [], [], []
        for p, e in heads:
            km = key_mask[e]
            lq = jnp.concatenate([blk(at_s, p), blk(rt_s, p)], axis=0)
            rk = jnp.where(km, jnp.concatenate([blk(bt_s, p), blk(kt_s, p)], axis=0), 0.0)
            aall.append(_mm_nt(lq, rk))
            vm.append(jnp.where(km, 0.0, blk(v_s, p)))
        for i, (p, e) in enumerate(heads):
            a_k = jnp.where(tri_r, aall[i], 0.0)
            av = _mm(a_k, jnp.concatenate([vm[i], vm[i]], axis=0))
            z.append(jnp.where(key_mask[e], blk(at_s, p), av[:CHUNK]))
            ry.append(jnp.where(key_mask[e], blk(rt_s, p), av[CHUNK:]))
        xp = [jnp.where(strict_l, aall[i][:CHUNK], 0.0) for i in range(N_HEADS)]
        for it in range(n_doubling):
            for i in range(N_HEADS):
                if it + 1 < n_doubling:
                    zx = _mm(xp[i][:, :CHUNK], jnp.concatenate([z[i], xp[i]], axis=1))
                    z[i] = z[i] + zx[:, :PAIR]
                    xp[i] = zx[:, PAIR:]
                else:
                    z[i] = z[i] + _mm(xp[i][:, :CHUNK], z[i])
        mgk = []
        for i, (p, e) in enumerate(heads):
            a_rb = jnp.where(incl_l, aall[i][CHUNK:], 0.0)
            ry[i] = ry[i] + _mm(a_rb[:, :CHUNK], z[i])
            mg = _mm_tn(jnp.concatenate([blk(bh_s, p), blk(kh_s, p)], axis=0),
                        jnp.concatenate([z[i], vm[i]], axis=0))
            mgk.append(mg[e * HEAD:(e + 1) * HEAD]
                       + jnp.where(eye_key[e], blk(ge_s, p)[0:1, :], 0.0))
        ys = []
        for i, (p, e) in enumerate(heads):
            sp = state[i]
            rhs = jnp.concatenate([sp, zeros_blk] if e == 0 else [zeros_blk, sp], axis=0)
            res = _mm(jnp.concatenate([ry[i], mgk[i]], axis=0), rhs)
            ys.append(res[:CHUNK] + ry[i])
            state[i] = res[CHUNK:] + jnp.where(key_mask[e], 0.0, mgk[i])
        for p in range(N_HEADS // 2):
            y_s[rows, p * PAIR:(p + 1) * PAIR] = jnp.where(key_mask[0], ys[2 * p + 1], ys[2 * p])
        return carry

    lax.fori_loop(0, TILE // CHUNK, chunk_body, 0)

    y = y_s[...]
    mean = _mm_exact_rhs(y, headsum, 2) * (1.0 / HEAD)
    d = y - mean
    var = _mm_exact_rhs(d * d, headsum, 2) * (1.0 / HEAD)
    y = d * lax.rsqrt(var + GN_EPS) * gng_ref[...] + gnb_ref[...]
    g_b = jnp.dot(h, win_ref[:, 2 * D_POOL + D_SHIFT:], preferred_element_type=F32)
    y_b = (y + bonus) * _silu(g_b)

    yy = jnp.concatenate([y_a, y_b], axis=1).astype(BF16)
    xo = x + jnp.dot(yy, wout_ref[...], preferred_element_type=F32)
    inv_o = lax.rsqrt(jnp.mean(xo * xo, axis=-1, keepdims=True) + NORM_EPS)
    o_ref[0] = xo * inv_o * fg_ref[...]


def _full(shape):
    return pl.BlockSpec(shape, lambda b, j: (0,) * len(shape))


def kernel(x, norm_gain, w_in, pool_w, pool_scale, shift_mu, w0, w_up, a0, a_up, k_k, k_a, r_k,
           gn_gain, gn_bias, w_out, final_gain):
    batch, seq, d_model = x.shape
    assert norm_gain.shape[0] == 1 and d_model == D_MODEL and seq % TILE == 0
    assert w_in.shape == (1, D_MODEL, D_IN)

    lora_w = jnp.zeros((2 * LORA, 2 * D_RWKV), F32)
    lora_w = lora_w.at[:LORA, :D_RWKV].set(w_up[0]).at[LORA:, D_RWKV:].set(a_up[0])
    ch = jnp.arange(D_RWKV)
    hid = ch // HEAD
    headsum = (hid[:, None] == hid[None, :]).astype(BF16)
    headsum_swap = (hid[:, None] == (hid[None, :] ^ 1)).astype(BF16)
    t = jnp.arange(TILE)
    same_chunk = (t[:, None] // CHUNK) == (t[None, :] // CHUNK)
    cumtri = (same_chunk & (t[:, None] >= t[None, :])).astype(BF16)
    chunkones = same_chunk.astype(BF16)
    row2 = lambda p: p.reshape(1, -1).astype(F32)

    vperm = (hid ^ 1) * HEAD + ch % HEAD
    v0 = 2 * D_POOL + 2 * D_RWKV
    g0 = 2 * D_POOL + D_SHIFT
    in_cols = jnp.arange(D_IN)
    in_cols = in_cols.at[v0:v0 + D_RWKV].set(v0 + vperm).at[g0:g0 + D_RWKV].set(g0 + vperm)
    sh_cols = jnp.arange(D_SHIFT).at[2 * D_RWKV:3 * D_RWKV].set(2 * D_RWKV + vperm)
    out_rows = jnp.arange(D_MODEL).at[D_POOL:].set(D_POOL + vperm)

    operands = (
        x, row2(norm_gain[0]), w_in[0][:, in_cols].astype(BF16), pool_w[0].astype(BF16),
        row2(pool_scale[0]), row2(shift_mu[0][sh_cols]), row2(w0[0]), lora_w, row2(a0[0]),
        row2(k_k[0]), row2(k_a[0]), row2(r_k[0]), row2(gn_gain[0][vperm]), row2(gn_bias[0][vperm]),
        w_out[0][out_rows].astype(BF16), row2(final_gain),
        headsum, headsum_swap, cumtri, chunkones,
    )
    in_specs = [pl.BlockSpec((1, TILE, D_MODEL), lambda b, j: (b, j, 0))]
    in_specs += [_full(op.shape) for op in operands[1:]]
    seq_buf = lambda: pltpu.VMEM((TILE, D_RWKV), F32)
    return pl.pallas_call(
        _block_kernel,
        grid=(batch, seq // TILE),
        in_specs=in_specs,
        out_specs=pl.BlockSpec((1, TILE, D_MODEL), lambda b, j: (b, j, 0)),
        out_shape=jax.ShapeDtypeStruct(x.shape, x.dtype),
        scratch_shapes=[
            pltpu.VMEM((1, D_SHIFT), F32),
            pltpu.VMEM((POOL_HALO, D_POOL), F32),
            pltpu.VMEM((N_HEADS, HEAD, PAIR), F32),
        ] + [seq_buf() for _ in range(9)],
        compiler_params=pltpu.CompilerParams(
            dimension_semantics=("arbitrary", "arbitrary"),
            vmem_limit_bytes=VMEM_LIMIT_BYTES),
        name="hybrid_pool_rwkv7_block",
    )(*operands)
```

```python
import math

import jax
import jax.numpy as jnp
from jax import lax
from jax.experimental import pallas as pl
from jax.experimental.pallas import tpu as pltpu

F32 = jnp.float32
BF16 = jnp.bfloat16

D_MODEL = 1024
D_POOL = 512
POOL_WINDOWS = (2, 4, 8, 16)
POOL_GROUP = 128
POOL_HALO = 16
D_RWKV = 512
N_HEADS = 8
HEAD = 64
PAIR = 2 * HEAD
LORA = 64
D_SHIFT = 3 * D_RWKV + 2 * LORA
D_IN = 2 * D_POOL + D_SHIFT + D_RWKV
NORM_EPS = 1e-6
GN_EPS = 64e-5
CHUNK = 64
TILE = 256
GROUP = 4
DECAY_SCALE = math.exp(-0.5)
VMEM_LIMIT_BYTES = 48 * 1024 * 1024


def _mm(a, b):
    return jnp.dot(a.astype(BF16), b.astype(BF16), preferred_element_type=F32)


def _mm_nt(a, b):
    return lax.dot_general(a.astype(BF16), b.astype(BF16), (((1,), (1,)), ((), ())),
                           preferred_element_type=F32)


def _split(a, parts):
    out = []
    rem = a
    for _ in range(parts):
        p = rem.astype(BF16)
        out.append(p)
        rem = rem - p.astype(F32)
    return out


def _mm_exact_lhs(a_bf16, b, parts):
    acc = None
    for p in _split(b, parts):
        t = jnp.dot(a_bf16, p, preferred_element_type=F32)
        acc = t if acc is None else acc + t
    return acc


def _aligned(i, m):
    return i if isinstance(i, int) else pl.multiple_of(i, m)


def _sigmoid(t):
    return 1.0 / (1.0 + jnp.exp(-t))


def _silu(t):
    return t * _sigmoid(t)


def _block_kernel(x_ref, ng_ref, win_ref, poolw_ref, pscale_ref, mu_ref, w0_ref, lora_ref,
                  a0_ref, kk_ref, ka_ref, rk_ref, gng_ref, gnb_ref, wout_ref, fg_ref,
                  headsum_ref, headswap_ref, cumtri_ref,
                  o_ref,
                  shift_carry, pool_carry, state,
                  at_s, rt_s, bt_s, kt_s, bh_s, kh_s, v_s, ge_s, y_s):
    j = pl.program_id(1)

    @pl.when(j == 0)
    def _():
        shift_carry[...] = jnp.zeros_like(shift_carry)
        pool_carry[...] = jnp.zeros_like(pool_carry)
        state[...] = jnp.zeros_like(state)

    x = x_ref[0]
    inv = lax.rsqrt(jnp.mean(x * x, axis=-1, keepdims=True) + NORM_EPS)
    h = (x * inv * ng_ref[...]).astype(BF16)

    row = lax.broadcasted_iota(jnp.int32, (TILE, 1), 0)

    z_pool = jnp.dot(h, win_ref[:, 0:2 * D_POOL], preferred_element_type=F32)
    u = z_pool[:, :D_POOL]
    g_a = z_pool[:, D_POOL:]
    ext = jnp.concatenate([pool_carry[...], u], axis=0)
    pool_carry[...] = u[TILE - POOL_HALO:, :]
    pos = (j * TILE + row).astype(F32)
    pooled = []
    s = ext
    for g, w in enumerate(POOL_WINDOWS):
        lo = g * POOL_GROUP
        s = s[:, (POOL_GROUP if g > 0 else 0):]
        s = s + pltpu.roll(s, w // 2, axis=0)
        inv_cnt = 1.0 / jnp.minimum(pos + 1.0, float(w))
        pooled.append(s[POOL_HALO:, :POOL_GROUP] * inv_cnt - u[:, lo:lo + POOL_GROUP])
    mixed = [_mm(pooled[g], poolw_ref[g]) for g in range(len(POOL_WINDOWS))]
    y_a = jnp.concatenate(mixed, axis=1) * pscale_ref[...] * _silu(g_a)

    z_sh = jnp.dot(h, win_ref[:, 2 * D_POOL:2 * D_POOL + D_SHIFT], preferred_element_type=F32)
    prev = pltpu.roll(z_sh, 1, axis=0)
    prev = jnp.where(row == 0, shift_carry[...], prev)
    shift_carry[...] = z_sh[TILE - 1:TILE, :]
    sh = z_sh + mu_ref[...] * (prev - z_sh)

    r = sh[:, 0:D_RWKV]
    k = sh[:, D_RWKV:2 * D_RWKV]
    v = sh[:, 2 * D_RWKV:3 * D_RWKV]
    lr = sh[:, 3 * D_RWKV:]
    lane = lax.broadcasted_iota(jnp.int32, (1, 2 * LORA), 1)
    lr = jnp.where(lane < LORA, jnp.tanh(lr), lr)
    lora = _mm(lr, lora_ref[...])
    ww = w0_ref[...] + lora[:, :D_RWKV]
    a = _sigmoid(a0_ref[...] + lora[:, D_RWKV:])
    logw = -DECAY_SCALE * _sigmoid(ww)

    headsum = headsum_ref[...]
    kkv = k * kk_ref[...]
    ss = _mm(kkv * kkv, headsum)
    kkn = kkv * lax.rsqrt(jnp.maximum(ss, 1e-24))
    k2 = k * (1.0 + (a - 1.0) * ka_ref[...])
    bonus = _mm(r * k2 * rk_ref[...], headswap_ref[...]) * v

    gcum = _mm_exact_lhs(cumtri_ref[...], logw, 2)
    e_inv = jnp.exp(-gcum)
    kka = kkn * a
    at_s[...] = -kkn * jnp.exp(gcum - logw)
    rt_s[...] = r * jnp.exp(gcum)
    bt_s[...] = kka * e_inv
    kt_s[...] = k2 * e_inv
    v_s[...] = v
    for c in range(TILE // CHUNK):
        last = (c + 1) * CHUNK - 1
        gam_c = jnp.exp(gcum[last:last + 1, :])
        ge_s[c:c + 1, :] = gam_c
        e_end = e_inv[c * CHUNK:(c + 1) * CHUNK] * gam_c
        bh_s[c * CHUNK:(c + 1) * CHUNK, :] = kka[c * CHUNK:(c + 1) * CHUNK] * e_end
        kh_s[c * CHUNK:(c + 1) * CHUNK, :] = k2[c * CHUNK:(c + 1) * CHUNK] * e_end

    ri = lax.broadcasted_iota(jnp.int32, (CHUNK, PAIR), 0)
    ci = lax.broadcasted_iota(jnp.int32, (CHUNK, PAIR), 1)
    cm = ci % CHUNK
    left = ci < CHUNK
    tri_k = jnp.concatenate([(ri > cm) & left, (ri >= cm) & left], axis=0)
    strict_b = (ri > cm) & ~left
    incl_b = (ri >= cm) & ~left
    eye_l = jnp.where(ri == ci, 1.0, 0.0)
    lane_half = lax.broadcasted_iota(jnp.int32, (1, PAIR), 1) // HEAD
    key_mask = [lane_half == e for e in range(2)]
    eye_key = [ci == ri + e * HEAD for e in range(2)]
    zeros_blk = jnp.zeros((HEAD, PAIR), F32)
    n_doubling = int(math.log2(CHUNK))
    heads = [(p, e) for p in range(N_HEADS // 2) for e in range(2)]
    chains = [(cc, p, e) for cc in range(GROUP) for p, e in heads]

    def group_body(g, carry):
        rows = [pl.ds(_aligned((g * GROUP + cc) * CHUNK, CHUNK), CHUNK) for cc in range(GROUP)]
        blk = lambda ref, cc, p: ref[rows[cc], p * PAIR:(p + 1) * PAIR]
        aall, vm, z, ry, tx, mgk = [], [], [], [], [], []
        for cc, p, e in chains:
            km = key_mask[e]
            lq = jnp.concatenate([blk(at_s, cc, p), blk(rt_s, cc, p)], axis=0)
            rk = jnp.where(km, jnp.concatenate([blk(kt_s, cc, p), blk(bt_s, cc, p)], axis=0), 0.0)
            aall.append(_mm_nt(lq, rk))
            vm.append(jnp.where(km, 0.0, blk(v_s, cc, p)))
        for i, (cc, p, e) in enumerate(chains):
            a_k = jnp.where(tri_k, aall[i], 0.0)
            av = _mm(a_k, jnp.concatenate([vm[i], vm[i]], axis=0))
            z.append(jnp.where(key_mask[e], blk(at_s, cc, p), av[:CHUNK]))
            ry.append(jnp.where(key_mask[e], blk(rt_s, cc, p), av[CHUNK:]))
            tx.append(jnp.where(strict_b, aall[i][:CHUNK], eye_l))
        for _ in range(n_doubling):
            for i in range(len(chains)):
                tx[i] = (jnp.where(left, tx[i], 0.0)
                         + _mm(tx[i], jnp.concatenate([zeros_blk, tx[i]], axis=0)))
        for i, (cc, p, e) in enumerate(chains):
            z[i] = _mm(tx[i][:, :CHUNK], z[i])
        for i, (cc, p, e) in enumerate(chains):
            a_rb = jnp.where(incl_b, aall[i][CHUNK:], 0.0)
            ry[i] = ry[i] + _mm(a_rb, jnp.concatenate([z[i], z[i]], axis=0))
            if e == 0:
                bk_t = jnp.concatenate([blk(bh_s, cc, p), blk(kh_s, cc, p)], axis=0).T
            mg = _mm(bk_t[e * HEAD:(e + 1) * HEAD], jnp.concatenate([z[i], vm[i]], axis=0))
            gam_c = ge_s[pl.ds(g * GROUP + cc, 1), p * PAIR:(p + 1) * PAIR]
            mgk.append(mg + jnp.where(eye_key[e], gam_c, 0.0))
        for cc in range(GROUP):
            ys = []
            for h, (p, e) in enumerate(heads):
                i = cc * N_HEADS + h
                sp = state[h]
                rhs = jnp.concatenate([sp, zeros_blk] if e == 0 else [zeros_blk, sp], axis=0)
                res = _mm(jnp.concatenate([ry[i], mgk[i]], axis=0), rhs)
                ys.append(res[:CHUNK] + ry[i])
                state[h] = res[CHUNK:] + jnp.where(key_mask[e], 0.0, mgk[i])
            for p in range(N_HEADS // 2):
                y_s[rows[cc], p * PAIR:(p + 1) * PAIR] = jnp.where(
                    key_mask[0], ys[2 * p + 1], ys[2 * p])
        return carry

    if TILE // CHUNK == GROUP:
        group_body(0, 0)
    else:
        lax.fori_loop(0, TILE // (CHUNK * GROUP), group_body, 0)

    y = y_s[...]
    mean = _mm(y, headsum) * (1.0 / HEAD)
    d = y - mean
    var = _mm(d * d, headsum) * (1.0 / HEAD)
    y = d * lax.rsqrt(var + GN_EPS) * gng_ref[...] + gnb_ref[...]
    g_b = jnp.dot(h, win_ref[:, 2 * D_POOL + D_SHIFT:], preferred_element_type=F32)
    y_b = (y + bonus) * _silu(g_b)

    yy = jnp.concatenate([y_a, y_b], axis=1).astype(BF16)
    xo = x + jnp.dot(yy, wout_ref[...], preferred_element_type=F32)
    inv_o = lax.rsqrt(jnp.mean(xo * xo, axis=-1, keepdims=True) + NORM_EPS)
    o_ref[0] = xo * inv_o * fg_ref[...]


def _full(shape):
    return pl.BlockSpec(shape, lambda b, j: (0,) * len(shape))


def kernel(x, norm_gain, w_in, pool_w, pool_scale, shift_mu, w0, w_up, a0, a_up, k_k, k_a, r_k,
           gn_gain, gn_bias, w_out, final_gain):
    batch, seq, d_model = x.shape
    assert norm_gain.shape[0] == 1 and d_model == D_MODEL and seq % TILE == 0
    assert w_in.shape == (1, D_MODEL, D_IN)

    lora_w = jnp.zeros((2 * LORA, 2 * D_RWKV), F32)
    lora_w = lora_w.at[:LORA, :D_RWKV].set(w_up[0]).at[LORA:, D_RWKV:].set(a_up[0])
    ch = jnp.arange(D_RWKV)
    hid = ch // HEAD
    headsum = (hid[:, None] == hid[None, :]).astype(BF16)
    headsum_swap = (hid[:, None] == (hid[None, :] ^ 1)).astype(BF16)
    t = jnp.arange(TILE)
    same_chunk = (t[:, None] // CHUNK) == (t[None, :] // CHUNK)
    cumtri = (same_chunk & (t[:, None] >= t[None, :])).astype(BF16)
    row2 = lambda p: p.reshape(1, -1).astype(F32)

    vperm = (hid ^ 1) * HEAD + ch % HEAD
    v0 = 2 * D_POOL + 2 * D_RWKV
    g0 = 2 * D_POOL + D_SHIFT
    in_cols = jnp.arange(D_IN)
    in_cols = in_cols.at[v0:v0 + D_RWKV].set(v0 + vperm).at[g0:g0 + D_RWKV].set(g0 + vperm)
    sh_cols = jnp.arange(D_SHIFT).at[2 * D_RWKV:3 * D_RWKV].set(2 * D_RWKV + vperm)
    out_rows = jnp.arange(D_MODEL).at[D_POOL:].set(D_POOL + vperm)

    operands = (
        x, row2(norm_gain[0]), w_in[0][:, in_cols].astype(BF16), pool_w[0].astype(BF16),
        row2(pool_scale[0]), row2(shift_mu[0][sh_cols]), row2(w0[0]), lora_w.astype(BF16),
        row2(a0[0]), row2(k_k[0]), row2(k_a[0]), row2(r_k[0]), row2(gn_gain[0][vperm]),
        row2(gn_bias[0][vperm]), w_out[0][out_rows].astype(BF16), row2(final_gain),
        headsum, headsum_swap, cumtri,
    )
    in_specs = [pl.BlockSpec((1, TILE, D_MODEL), lambda b, j: (b, j, 0))]
    in_specs += [_full(op.shape) for op in operands[1:]]
    seq_buf = lambda: pltpu.VMEM((TILE, D_RWKV), F32)
    return pl.pallas_call(
        _block_kernel,
        grid=(batch, seq // TILE),
        in_specs=in_specs,
        out_specs=pl.BlockSpec((1, TILE, D_MODEL), lambda b, j: (b, j, 0)),
        out_shape=jax.ShapeDtypeStruct(x.shape, x.dtype),
        scratch_shapes=[
            pltpu.VMEM((1, D_SHIFT), F32),
            pltpu.VMEM((POOL_HALO, D_POOL), F32),
            pltpu.VMEM((N_HEADS, HEAD, PAIR), F32),
        ] + [seq_buf() for _ in range(7)] + [pltpu.VMEM((8, D_RWKV), F32), seq_buf()],
        compiler_params=pltpu.CompilerParams(
            dimension_semantics=("arbitrary", "arbitrary"),
            vmem_limit_bytes=VMEM_LIMIT_BYTES),
        name="hybrid_pool_rwkv7_block",
    )(*operands)
```

```python
import math

import jax
import jax.numpy as jnp
from jax import lax
from jax.experimental import pallas as pl
from jax.experimental.pallas import tpu as pltpu

F32 = jnp.float32
BF16 = jnp.bfloat16

D_MODEL = 1024
D_POOL = 512
POOL_WINDOWS = (2, 4, 8, 16)
POOL_GROUP = 128
POOL_HALO = 16
D_RWKV = 512
N_HEADS = 8
HEAD = 64
PAIR = 2 * HEAD
LORA = 64
D_SHIFT = 3 * D_RWKV + 2 * LORA
D_IN = 2 * D_POOL + D_SHIFT + D_RWKV
NORM_EPS = 1e-6
GN_EPS = 64e-5
CHUNK = 64
TILE = 256
GROUP = 4
DECAY_SCALE = math.exp(-0.5)
VMEM_LIMIT_BYTES = 56 * 1024 * 1024


def _mm(a, b):
    return jnp.dot(a.astype(BF16), b.astype(BF16), preferred_element_type=F32)


def _split(a, parts):
    out = []
    rem = a
    for _ in range(parts):
        p = rem.astype(BF16)
        out.append(p)
        rem = rem - p.astype(F32)
    return out


def _mm_exact_lhs(a_bf16, b, parts):
    acc = None
    for p in _split(b, parts):
        t = jnp.dot(a_bf16, p, preferred_element_type=F32)
        acc = t if acc is None else acc + t
    return acc


def _aligned(i, m):
    return i if isinstance(i, int) else pl.multiple_of(i, m)


def _sigmoid(t):
    return 1.0 / (1.0 + jnp.exp(-t))


def _silu(t):
    return t * _sigmoid(t)


def _swap_pair_lanes(a):
    rows = a.shape[0]
    if rows < 8:
        a = jnp.broadcast_to(a[0:1], (8, a.shape[1]))
    out = jnp.concatenate([pltpu.roll(a[:, c:c + PAIR], HEAD, axis=1)
                           for c in range(0, a.shape[1], PAIR)], axis=1)
    return out[:rows]


def _prepare_weights(win_f, poolw_f, mu_f, wup_f, aup_f, gng_f, gnb_f, wout_f,
                     win_ref, poolw_ref, mu_ref, lora_ref, gng_ref, gnb_ref, wout_ref):
    v0 = 2 * D_POOL + 2 * D_RWKV
    g0 = 2 * D_POOL + D_SHIFT
    for c in range(0, D_IN, PAIR):
        w = win_f[0, :, c:c + PAIR]
        if v0 <= c < v0 + D_RWKV or c >= g0:
            w = pltpu.roll(w, HEAD, axis=1)
        win_ref[:, c:c + PAIR] = w.astype(BF16)
    wout_ref[0:D_POOL, :] = wout_f[0, 0:D_POOL, :].astype(BF16)
    for r in range(D_POOL, D_MODEL, PAIR):
        wout_ref[r:r + HEAD, :] = wout_f[0, r + HEAD:r + PAIR, :].astype(BF16)
        wout_ref[r + HEAD:r + PAIR, :] = wout_f[0, r:r + HEAD, :].astype(BF16)
    poolw_ref[...] = poolw_f[0].astype(BF16)
    lora_ref[...] = jnp.zeros_like(lora_ref)
    lora_ref[0:LORA, 0:D_RWKV] = wup_f[0].astype(BF16)
    lora_ref[LORA:2 * LORA, D_RWKV:2 * D_RWKV] = aup_f[0].astype(BF16)
    mu_ref[...] = mu_f[...]
    mu_ref[:, 2 * D_RWKV:3 * D_RWKV] = _swap_pair_lanes(mu_f[:, 2 * D_RWKV:3 * D_RWKV])
    gng_ref[...] = _swap_pair_lanes(gng_f[...])
    gnb_ref[...] = _swap_pair_lanes(gnb_f[...])


def _block_kernel(x_ref, ng_ref, win_f, poolw_f, pscale_ref, mu_f, w0_ref, wup_f,
                  a0_ref, aup_f, kk_ref, ka_ref, rk_ref, gng_f, gnb_f, wout_f, fg_ref,
                  headsum_ref, headswap_ref, cumtri_ref,
                  o_ref,
                  shift_carry, pool_carry, state,
                  at_s, rt_s, bt_s, kt_s, bh_s, kh_s, v_s, ge_s, y_s,
                  win_ref, poolw_ref, mu_ref, lora_ref, gng_ref, gnb_ref, wout_ref):
    j = pl.program_id(1)

    @pl.when((pl.program_id(0) == 0) & (j == 0))
    def _():
        _prepare_weights(win_f, poolw_f, mu_f, wup_f, aup_f, gng_f, gnb_f, wout_f,
                         win_ref, poolw_ref, mu_ref, lora_ref, gng_ref, gnb_ref, wout_ref)

    @pl.when(j == 0)
    def _():
        shift_carry[...] = jnp.zeros_like(shift_carry)
        pool_carry[...] = jnp.zeros_like(pool_carry)
        state[...] = jnp.zeros_like(state)

    x = x_ref[0]
    inv = lax.rsqrt(jnp.mean(x * x, axis=-1, keepdims=True) + NORM_EPS)
    h = (x * inv * ng_ref[...]).astype(BF16)

    row = lax.broadcasted_iota(jnp.int32, (TILE, 1), 0)

    z_pool = jnp.dot(h, win_ref[:, 0:2 * D_POOL], preferred_element_type=F32)
    u = z_pool[:, :D_POOL]
    g_a = z_pool[:, D_POOL:]
    ext = jnp.concatenate([pool_carry[...], u], axis=0)
    pool_carry[...] = u[TILE - POOL_HALO:, :]
    pos = (j * TILE + row).astype(F32)
    pooled = []
    s = ext
    for g, w in enumerate(POOL_WINDOWS):
        lo = g * POOL_GROUP
        s = s[:, (POOL_GROUP if g > 0 else 0):]
        s = s + pltpu.roll(s, w // 2, axis=0)
        inv_cnt = 1.0 / jnp.minimum(pos + 1.0, float(w))
        pooled.append(s[POOL_HALO:, :POOL_GROUP] * inv_cnt - u[:, lo:lo + POOL_GROUP])
    mixed = [_mm(pooled[g], poolw_ref[g]) for g in range(len(POOL_WINDOWS))]
    y_a = jnp.concatenate(mixed, axis=1) * pscale_ref[...] * _silu(g_a)

    z_sh = jnp.dot(h, win_ref[:, 2 * D_POOL:2 * D_POOL + D_SHIFT], preferred_element_type=F32)
    prev = pltpu.roll(z_sh, 1, axis=0)
    prev = jnp.where(row == 0, shift_carry[...], prev)
    shift_carry[...] = z_sh[TILE - 1:TILE, :]
    sh = z_sh + mu_ref[...] * (prev - z_sh)

    r = sh[:, 0:D_RWKV]
    k = sh[:, D_RWKV:2 * D_RWKV]
    v = sh[:, 2 * D_RWKV:3 * D_RWKV]
    lr = sh[:, 3 * D_RWKV:]
    lane = lax.broadcasted_iota(jnp.int32, (1, 2 * LORA), 1)
    lr = jnp.where(lane < LORA, jnp.tanh(lr), lr)
    lora = _mm(lr, lora_ref[...])
    ww = w0_ref[...] + lora[:, :D_RWKV]
    a = _sigmoid(a0_ref[...] + lora[:, D_RWKV:])
    logw = -DECAY_SCALE * _sigmoid(ww)

    headsum = headsum_ref[...]
    kkv = k * kk_ref[...]
    ss = _mm(kkv * kkv, headsum)
    kkn = kkv * lax.rsqrt(jnp.maximum(ss, 1e-24))
    k2 = k * (1.0 + (a - 1.0) * ka_ref[...])
    bonus = _mm(r * k2 * rk_ref[...], headswap_ref[...]) * v

    gcum = _mm_exact_lhs(cumtri_ref[...], logw, 2)
    e_inv = jnp.exp(-gcum)
    kka = kkn * a
    at_s[...] = -kkn * jnp.exp(gcum - logw)
    rt_s[...] = r * jnp.exp(gcum)
    bt_s[...] = kka * e_inv
    kt_s[...] = k2 * e_inv
    v_s[...] = v
    for c in range(TILE // CHUNK):
        last = (c + 1) * CHUNK - 1
        gam_c = jnp.exp(gcum[last:last + 1, :])
        ge_s[c:c + 1, :] = gam_c
        e_end = e_inv[c * CHUNK:(c + 1) * CHUNK] * gam_c
        bh_s[c * CHUNK:(c + 1) * CHUNK, :] = kka[c * CHUNK:(c + 1) * CHUNK] * e_end
        kh_s[c * CHUNK:(c + 1) * CHUNK, :] = k2[c * CHUNK:(c + 1) * CHUNK] * e_end

    ri = lax.broadcasted_iota(jnp.int32, (CHUNK, PAIR), 0)
    ci = lax.broadcasted_iota(jnp.int32, (CHUNK, PAIR), 1)
    cm = ci % CHUNK
    left = ci < CHUNK
    tri_k = jnp.concatenate([(ri > cm) & ~left, (ri >= cm) & ~left], axis=0)
    strict_b = (ri > cm) & left
    incl_b = (ri >= cm) & left
    eye_r = jnp.where(ci == ri + CHUNK, 1.0, 0.0)
    lane_half = lax.broadcasted_iota(jnp.int32, (1, PAIR), 1) // HEAD
    key_mask = [lane_half == e for e in range(2)]
    eye_key = [ci == ri + e * HEAD for e in range(2)]
    zeros_blk = jnp.zeros((HEAD, PAIR), F32)
    n_doubling = int(math.log2(CHUNK))
    heads = [(p, e) for p in range(N_HEADS // 2) for e in range(2)]
    chains = [(cc, p, e) for cc in range(GROUP) for p, e in heads]

    def group_body(g, carry):
        rows = [pl.ds(_aligned((g * GROUP + cc) * CHUNK, CHUNK), CHUNK) for cc in range(GROUP)]
        blk = lambda ref, cc, p: ref[rows[cc], p * PAIR:(p + 1) * PAIR]
        aall, vm, z, ry, tx, mgk = [], [], [], [], [], []
        for cc, p, e in chains:
            if e == 0:
                lq = jnp.concatenate([blk(at_s, cc, p), blk(rt_s, cc, p)], axis=0)
                rk_t = jnp.concatenate([blk(bt_s, cc, p), blk(kt_s, cc, p)], axis=0).T
                aall.append(_mm(lq[:, :HEAD], rk_t[:HEAD]))
            else:
                aall.append(_mm(lq, jnp.concatenate([zeros_blk, rk_t[HEAD:]], axis=0)))
            vm.append(jnp.where(key_mask[e], 0.0, blk(v_s, cc, p)))
        for i, (cc, p, e) in enumerate(chains):
            a_k = jnp.where(tri_k, aall[i], 0.0)
            av = _mm(a_k, jnp.concatenate([vm[i], vm[i]], axis=0))
            z.append(jnp.where(key_mask[e], blk(at_s, cc, p), av[:CHUNK]))
            ry.append(jnp.where(key_mask[e], blk(rt_s, cc, p), av[CHUNK:]))
            tx.append(jnp.where(strict_b, aall[i][:CHUNK], eye_r))
        for _ in range(n_doubling):
            for i in range(len(chains)):
                tx[i] = jnp.where(left, 0.0, tx[i]) + _mm(tx[i][:, :CHUNK], tx[i])
        for i, (cc, p, e) in enumerate(chains):
            t_inv = jnp.where(left, 0.0, tx[i])
            z[i] = _mm(t_inv, jnp.concatenate([z[i], z[i]], axis=0))
        for i, (cc, p, e) in enumerate(chains):
            a_rb = jnp.where(incl_b, aall[i][CHUNK:], 0.0)
            ry[i] = ry[i] + _mm(a_rb[:, :CHUNK], z[i])
            if e == 0:
                bk_t = jnp.concatenate([blk(bh_s, cc, p), blk(kh_s, cc, p)], axis=0).T
            mg = _mm(bk_t[e * HEAD:(e + 1) * HEAD], jnp.concatenate([z[i], vm[i]], axis=0))
            gam_c = ge_s[pl.ds(g * GROUP + cc, 1), p * PAIR:(p + 1) * PAIR]
            mgk.append(mg + jnp.where(eye_key[e], gam_c, 0.0))
        for cc in range(GROUP):
            ys = []
            for h, (p, e) in enumerate(heads):
                i = cc * N_HEADS + h
                sp = state[h]
                lhs = jnp.concatenate([ry[i], mgk[i]], axis=0)
                if e == 0:
                    res = _mm(lhs[:, :HEAD], sp)
                else:
                    res = _mm(lhs, jnp.concatenate([zeros_blk, sp], axis=0))
                ys.append(res[:CHUNK] + ry[i])
                state[h] = res[CHUNK:] + jnp.where(key_mask[e], 0.0, mgk[i])
            for p in range(N_HEADS // 2):
                y_s[rows[cc], p * PAIR:(p + 1) * PAIR] = jnp.where(
                    key_mask[0], ys[2 * p + 1], ys[2 * p])
        return carry

    if TILE // CHUNK == GROUP:
        group_body(0, 0)
    else:
        lax.fori_loop(0, TILE // (CHUNK * GROUP), group_body, 0)

    y = y_s[...]
    mean = _mm(y, headsum) * (1.0 / HEAD)
    d = y - mean
    var = _mm(d * d, headsum) * (1.0 / HEAD)
    y = d * lax.rsqrt(var + GN_EPS) * gng_ref[...] + gnb_ref[...]
    g_b = jnp.dot(h, win_ref[:, 2 * D_POOL + D_SHIFT:], preferred_element_type=F32)
    y_b = (y + bonus) * _silu(g_b)

    yy = jnp.concatenate([y_a, y_b], axis=1).astype(BF16)
    xo = x + jnp.dot(yy, wout_ref[...], preferred_element_type=F32)
    inv_o = lax.rsqrt(jnp.mean(xo * xo, axis=-1, keepdims=True) + NORM_EPS)
    o_ref[0] = xo * inv_o * fg_ref[...]


def _full(shape, single_buffer=False):
    index_map = lambda b, j: (0,) * len(shape)
    if single_buffer:
        return pl.BlockSpec(shape, index_map, pipeline_mode=pl.Buffered(1))
    return pl.BlockSpec(shape, index_map)


def kernel(x, norm_gain, w_in, pool_w, pool_scale, shift_mu, w0, w_up, a0, a_up, k_k, k_a, r_k,
           gn_gain, gn_bias, w_out, final_gain):
    batch, seq, d_model = x.shape
    assert norm_gain.shape[0] == 1 and d_model == D_MODEL and seq % TILE == 0
    assert w_in.shape == (1, D_MODEL, D_IN)

    hid = jnp.arange(D_RWKV) // HEAD
    headsum = (hid[:, None] == hid[None, :]).astype(BF16)
    headsum_swap = (hid[:, None] == (hid[None, :] ^ 1)).astype(BF16)
    t = jnp.arange(TILE)
    same_chunk = (t[:, None] // CHUNK) == (t[None, :] // CHUNK)
    cumtri = (same_chunk & (t[:, None] >= t[None, :])).astype(BF16)

    operands = (
        x, norm_gain, w_in, pool_w, pool_scale, shift_mu, w0, w_up, a0, a_up, k_k, k_a,
        r_k.reshape(1, D_RWKV), gn_gain, gn_bias, w_out, final_gain.reshape(1, D_MODEL),
        headsum, headsum_swap, cumtri,
    )
    big = {2, 15}
    in_specs = [pl.BlockSpec((1, TILE, D_MODEL), lambda b, j: (b, j, 0))]
    in_specs += [_full(op.shape, i in big) for i, op in enumerate(operands) if i > 0]
    seq_buf = lambda: pltpu.VMEM((TILE, D_RWKV), F32)
    return pl.pallas_call(
        _block_kernel,
        grid=(batch, seq // TILE),
        in_specs=in_specs,
        out_specs=pl.BlockSpec((1, TILE, D_MODEL), lambda b, j: (b, j, 0)),
        out_shape=jax.ShapeDtypeStruct(x.shape, x.dtype),
        scratch_shapes=[
            pltpu.VMEM((1, D_SHIFT), F32),
            pltpu.VMEM((POOL_HALO, D_POOL), F32),
            pltpu.VMEM((N_HEADS, HEAD, PAIR), F32),
        ] + [seq_buf() for _ in range(7)] + [pltpu.VMEM((8, D_RWKV), F32), seq_buf()] + [
            pltpu.VMEM((D_MODEL, D_IN), BF16),
            pltpu.VMEM((len(POOL_WINDOWS), POOL_GROUP, POOL_GROUP), BF16),
            pltpu.VMEM((1, D_SHIFT), F32),
            pltpu.VMEM((2 * LORA, 2 * D_RWKV), BF16),
            pltpu.VMEM((1, D_RWKV), F32),
            pltpu.VMEM((1, D_RWKV), F32),
            pltpu.VMEM((D_MODEL, D_MODEL), BF16),
        ],
        compiler_params=pltpu.CompilerParams(
            dimension_semantics=("arbitrary", "arbitrary"),
            vmem_limit_bytes=VMEM_LIMIT_BYTES),
        name="hybrid_pool_rwkv7_block",
    )(*operands)
```

```python
import math

import jax
import jax.numpy as jnp
from jax import lax
from jax.experimental import pallas as pl
from jax.experimental.pallas import tpu as pltpu

F32 = jnp.float32
BF16 = jnp.bfloat16

D_MODEL = 1024
D_POOL = 512
POOL_WINDOWS = (2, 4, 8, 16)
POOL_GROUP = 128
POOL_HALO = 16
D_RWKV = 512
N_HEADS = 8
HEAD = 64
PAIR = 2 * HEAD
LORA = 64
D_SHIFT = 3 * D_RWKV + 2 * LORA
D_IN = 2 * D_POOL + D_SHIFT + D_RWKV
NORM_EPS = 1e-6
GN_EPS = 64e-5
CHUNK = 64
TILE = 256
N_CHUNKS = TILE // CHUNK
DECAY_SCALE = math.exp(-0.5)
VMEM_LIMIT_BYTES = 56 * 1024 * 1024


def _mm(a, b):
    return jnp.dot(a.astype(BF16), b.astype(BF16), preferred_element_type=F32)


def _split(a, parts):
    out = []
    rem = a
    for _ in range(parts):
        p = rem.astype(BF16)
        out.append(p)
        rem = rem - p.astype(F32)
    return out


def _mm_exact_lhs(a_bf16, b, parts):
    acc = None
    for p in _split(b, parts):
        t = jnp.dot(a_bf16, p, preferred_element_type=F32)
        acc = t if acc is None else acc + t
    return acc


def _sigmoid(t):
    return 0.5 * jnp.tanh(0.5 * t) + 0.5


def _silu(t):
    return t * _sigmoid(t)


def _swap_pair_lanes(a):
    rows = a.shape[0]
    if rows < 8:
        a = jnp.broadcast_to(a[0:1], (8, a.shape[1]))
    out = jnp.concatenate([pltpu.roll(a[:, c:c + PAIR], HEAD, axis=1)
                           for c in range(0, a.shape[1], PAIR)], axis=1)
    return out[:rows]


def _prepare_weights(win_f, poolw_f, mu_f, wup_f, aup_f, gng_f, gnb_f, wout_f,
                     win_ref, poolw_ref, mu_ref, lora_ref, gng_ref, gnb_ref, wout_ref):
    v0 = 2 * D_POOL + 2 * D_RWKV
    g0 = 2 * D_POOL + D_SHIFT
    for c in range(0, D_IN, PAIR):
        w = win_f[0, :, c:c + PAIR]
        if v0 <= c < v0 + D_RWKV or c >= g0:
            w = pltpu.roll(w, HEAD, axis=1)
        win_ref[:, c:c + PAIR] = w.astype(BF16)
    wout_ref[0:D_POOL, :] = wout_f[0, 0:D_POOL, :].astype(BF16)
    for r in range(D_POOL, D_MODEL, PAIR):
        wout_ref[r:r + HEAD, :] = wout_f[0, r + HEAD:r + PAIR, :].astype(BF16)
        wout_ref[r + HEAD:r + PAIR, :] = wout_f[0, r:r + HEAD, :].astype(BF16)
    poolw_ref[...] = poolw_f[0].astype(BF16)
    lora_ref[...] = jnp.zeros_like(lora_ref)
    lora_ref[0:LORA, 0:D_RWKV] = wup_f[0].astype(BF16)
    lora_ref[LORA:2 * LORA, D_RWKV:2 * D_RWKV] = aup_f[0].astype(BF16)
    mu_ref[...] = mu_f[...]
    mu_ref[:, 2 * D_RWKV:3 * D_RWKV] = _swap_pair_lanes(mu_f[:, 2 * D_RWKV:3 * D_RWKV])
    gng_ref[...] = _swap_pair_lanes(gng_f[...])
    gnb_ref[...] = _swap_pair_lanes(gnb_f[...])


def _block_kernel(x_ref, ng_ref, win_f, poolw_f, pscale_ref, mu_f, w0_ref, wup_f,
                  a0_ref, aup_f, kk_ref, ka_ref, rk_ref, gng_f, gnb_f, wout_f, fg_ref,
                  headsum_ref, headswap_ref, cumtri_ref,
                  o_ref,
                  shift_carry, pool_carry, state,
                  at_s, rt_s, bt_s, kt_s, bh_s, kh_s, v_s, ge_s,
                  win_ref, poolw_ref, mu_ref, lora_ref, gng_ref, gnb_ref, wout_ref):
    j = pl.program_id(1)

    @pl.when((pl.program_id(0) == 0) & (j == 0))
    def _():
        _prepare_weights(win_f, poolw_f, mu_f, wup_f, aup_f, gng_f, gnb_f, wout_f,
                         win_ref, poolw_ref, mu_ref, lora_ref, gng_ref, gnb_ref, wout_ref)

    @pl.when(j == 0)
    def _():
        shift_carry[...] = jnp.zeros_like(shift_carry)
        pool_carry[...] = jnp.zeros_like(pool_carry)
        state[...] = jnp.zeros_like(state)

    x = x_ref[0]
    inv = lax.rsqrt(jnp.mean(x * x, axis=-1, keepdims=True) + NORM_EPS)
    h = (x * inv * ng_ref[...]).astype(BF16)

    row = lax.broadcasted_iota(jnp.int32, (TILE, 1), 0)

    z_pool = jnp.dot(h, win_ref[:, 0:2 * D_POOL], preferred_element_type=F32)
    u = z_pool[:, :D_POOL]
    g_a = z_pool[:, D_POOL:]
    ext = jnp.concatenate([pool_carry[...], u], axis=0)
    pool_carry[...] = u[TILE - POOL_HALO:, :]
    pos = (j * TILE + row).astype(F32)
    pooled = []
    s = ext
    for g, w in enumerate(POOL_WINDOWS):
        lo = g * POOL_GROUP
        s = s[:, (POOL_GROUP if g > 0 else 0):]
        s = s + pltpu.roll(s, w // 2, axis=0)
        inv_cnt = 1.0 / jnp.minimum(pos + 1.0, float(w))
        pooled.append(s[POOL_HALO:, :POOL_GROUP] * inv_cnt - u[:, lo:lo + POOL_GROUP])
    mixed = [_mm(pooled[g], poolw_ref[g]) for g in range(len(POOL_WINDOWS))]
    y_a = jnp.concatenate(mixed, axis=1) * pscale_ref[...] * _silu(g_a)

    z_sh = jnp.dot(h, win_ref[:, 2 * D_POOL:2 * D_POOL + D_SHIFT], preferred_element_type=F32)
    prev = pltpu.roll(z_sh, 1, axis=0)
    prev = jnp.where(row == 0, shift_carry[...], prev)
    shift_carry[...] = z_sh[TILE - 1:TILE, :]
    sh = z_sh + mu_ref[...] * (prev - z_sh)

    r = sh[:, 0:D_RWKV]
    k = sh[:, D_RWKV:2 * D_RWKV]
    v = sh[:, 2 * D_RWKV:3 * D_RWKV]
    lr = sh[:, 3 * D_RWKV:]
    lane = lax.broadcasted_iota(jnp.int32, (1, 2 * LORA), 1)
    lr = jnp.where(lane < LORA, jnp.tanh(lr), lr)
    lora = _mm(lr, lora_ref[...])
    ww = w0_ref[...] + lora[:, :D_RWKV]
    a = _sigmoid(a0_ref[...] + lora[:, D_RWKV:])
    logw = -DECAY_SCALE * _sigmoid(ww)

    headsum = headsum_ref[...]
    kkv = k * kk_ref[...]
    ss = _mm(kkv * kkv, headsum)
    kkn = kkv * lax.rsqrt(jnp.maximum(ss, 1e-24))
    k2 = k * (1.0 + (a - 1.0) * ka_ref[...])
    bonus = _mm(r * k2 * rk_ref[...], headswap_ref[...]) * v

    gcum = _mm_exact_lhs(cumtri_ref[...], logw, 2)
    e_inv = jnp.exp(-gcum)
    kka = kkn * a
    at_s[...] = -kkn * jnp.exp(gcum - logw)
    rt_s[...] = r * jnp.exp(gcum)
    bt_s[...] = kka * e_inv
    kt_s[...] = k2 * e_inv
    v_s[...] = v
    for c in range(N_CHUNKS):
        rows = slice(c * CHUNK, (c + 1) * CHUNK)
        gam_c = jnp.exp(gcum[rows.stop - 1:rows.stop, :])
        ge_s[c:c + 1, :] = gam_c
        e_end = e_inv[rows] * gam_c
        bh_s[rows, :] = kka[rows] * e_end
        kh_s[rows, :] = k2[rows] * e_end

    ri = lax.broadcasted_iota(jnp.int32, (CHUNK, PAIR), 0)
    ci = lax.broadcasted_iota(jnp.int32, (CHUNK, PAIR), 1)
    cm = ci % CHUNK
    left = ci < CHUNK
    tri_k = jnp.concatenate([(ri > cm) & ~left, (ri >= cm) & ~left], axis=0)
    strict_b = (ri > cm) & left
    incl_b = (ri >= cm) & left
    eye_r = jnp.where(ci == ri + CHUNK, 1.0, 0.0)
    lane_half = lax.broadcasted_iota(jnp.int32, (1, PAIR), 1) // HEAD
    key_mask = [lane_half == e for e in range(2)]
    eye_key = [ci == ri + e * HEAD for e in range(2)]
    zeros_blk = jnp.zeros((HEAD, PAIR), F32)
    keep_right = jnp.where(lane_half == 1, 1.0, 0.0)
    n_doubling = int(math.log2(CHUNK))
    heads = [(p, e) for p in range(N_HEADS // 2) for e in range(2)]
    chains = [(c, p, e) for c in range(N_CHUNKS) for p, e in heads]
    blk = lambda ref, c, p: ref[c * CHUNK:(c + 1) * CHUNK, p * PAIR:(p + 1) * PAIR]

    aall, vm, z, ry, tx, mgk = [], [], [], [], [], []
    for c, p, e in chains:
        if e == 0:
            lq = jnp.concatenate([blk(at_s, c, p), blk(rt_s, c, p)], axis=0)
            rk_t = jnp.concatenate([blk(bt_s, c, p), blk(kt_s, c, p)], axis=0).T
            aall.append(_mm(lq[:, :HEAD], rk_t[:HEAD]))
        else:
            aall.append(_mm(lq, jnp.concatenate([zeros_blk, rk_t[HEAD:]], axis=0)))
        vm.append(jnp.where(key_mask[e], 0.0, blk(v_s, c, p)))
    for i in range(len(chains)):
        tx.append(jnp.where(strict_b, aall[i][:CHUNK], eye_r))
    per_level = -(-len(chains) // n_doubling)
    for level in range(n_doubling):
        for i in range(len(chains)):
            tx[i] = tx[i] * keep_right + _mm(tx[i][:, :CHUNK], tx[i])
        for i, (c, p, e) in list(enumerate(chains))[level * per_level:(level + 1) * per_level]:
            a_k = jnp.where(tri_k, aall[i], 0.0)
            av = _mm(a_k, jnp.concatenate([vm[i], vm[i]], axis=0))
            z.append(jnp.where(key_mask[e], blk(at_s, c, p), av[:CHUNK]))
            ry.append(jnp.where(key_mask[e], blk(rt_s, c, p), av[CHUNK:]))
    for i in range(len(chains)):
        t_inv = tx[i] * keep_right
        z[i] = _mm(t_inv, jnp.concatenate([z[i], z[i]], axis=0))
    for i, (c, p, e) in enumerate(chains):
        a_rb = jnp.where(incl_b, aall[i][CHUNK:], 0.0)
        ry[i] = ry[i] + _mm(a_rb[:, :CHUNK], z[i])
        if e == 0:
            bk_t = jnp.concatenate([blk(bh_s, c, p), blk(kh_s, c, p)], axis=0).T
        mg = _mm(bk_t[e * HEAD:(e + 1) * HEAD], jnp.concatenate([z[i], vm[i]], axis=0))
        gam_c = ge_s[c:c + 1, p * PAIR:(p + 1) * PAIR]
        mgk.append(mg + jnp.where(eye_key[e], gam_c, 0.0))
    y_blocks = []
    for c in range(N_CHUNKS):
        ys = []
        for hd, (p, e) in enumerate(heads):
            i = c * N_HEADS + hd
            sp = state[hd]
            lhs = jnp.concatenate([ry[i], mgk[i]], axis=0)
            if e == 0:
                res = _mm(lhs[:, :HEAD], sp)
            else:
                res = _mm(lhs, jnp.concatenate([zeros_blk, sp], axis=0))
            ys.append(res[:CHUNK] + ry[i])
            state[hd] = res[CHUNK:] + jnp.where(key_mask[e], 0.0, mgk[i])
        y_blocks.append(jnp.concatenate(
            [jnp.where(key_mask[0], ys[2 * p + 1], ys[2 * p]) for p in range(N_HEADS // 2)],
            axis=1))
    y = jnp.concatenate(y_blocks, axis=0)

    mean = _mm(y, headsum) * (1.0 / HEAD)
    d = y - mean
    var = _mm(d * d, headsum) * (1.0 / HEAD)
    y = d * lax.rsqrt(var + GN_EPS) * gng_ref[...] + gnb_ref[...]
    g_b = jnp.dot(h, win_ref[:, 2 * D_POOL + D_SHIFT:], preferred_element_type=F32)
    y_b = (y + bonus) * _silu(g_b)

    yy = jnp.concatenate([y_a, y_b], axis=1).astype(BF16)
    xo = x + jnp.dot(yy, wout_ref[...], preferred_element_type=F32)
    inv_o = lax.rsqrt(jnp.mean(xo * xo, axis=-1, keepdims=True) + NORM_EPS)
    o_ref[0] = xo * inv_o * fg_ref[...]


def _full(shape, single_buffer=False):
    index_map = lambda b, j: (0,) * len(shape)
    if single_buffer:
        return pl.BlockSpec(shape, index_map, pipeline_mode=pl.Buffered(1))
    return pl.BlockSpec(shape, index_map)


def kernel(x, norm_gain, w_in, pool_w, pool_scale, shift_mu, w0, w_up, a0, a_up, k_k, k_a, r_k,
           gn_gain, gn_bias, w_out, final_gain):
    batch, seq, d_model = x.shape
    assert norm_gain.shape[0] == 1 and d_model == D_MODEL and seq % TILE == 0
    assert w_in.shape == (1, D_MODEL, D_IN)

    hid = jnp.arange(D_RWKV) // HEAD
    headsum = (hid[:, None] == hid[None, :]).astype(BF16)
    headsum_swap = (hid[:, None] == (hid[None, :] ^ 1)).astype(BF16)
    t = jnp.arange(TILE)
    same_chunk = (t[:, None] // CHUNK) == (t[None, :] // CHUNK)
    cumtri = (same_chunk & (t[:, None] >= t[None, :])).astype(BF16)

    operands = (
        x, norm_gain, w_in, pool_w, pool_scale, shift_mu, w0, w_up, a0, a_up, k_k, k_a,
        r_k.reshape(1, D_RWKV), gn_gain, gn_bias, w_out, final_gain.reshape(1, D_MODEL),
        headsum, headsum_swap, cumtri,
    )
    big = {2, 15}
    in_specs = [pl.BlockSpec((1, TILE, D_MODEL), lambda b, j: (b, j, 0))]
    in_specs += [_full(op.shape, i in big) for i, op in enumerate(operands) if i > 0]
    seq_buf = lambda: pltpu.VMEM((TILE, D_RWKV), F32)
    return pl.pallas_call(
        _block_kernel,
        grid=(batch, seq // TILE),
        in_specs=in_specs,
        out_specs=pl.BlockSpec((1, TILE, D_MODEL), lambda b, j: (b, j, 0)),
        out_shape=jax.ShapeDtypeStruct(x.shape, x.dtype),
        scratch_shapes=[
            pltpu.VMEM((1, D_SHIFT), F32),
            pltpu.VMEM((POOL_HALO, D_POOL), F32),
            pltpu.VMEM((N_HEADS, HEAD, PAIR), F32),
        ] + [seq_buf() for _ in range(7)] + [
            pltpu.VMEM((8, D_RWKV), F32),
            pltpu.VMEM((D_MODEL, D_IN), BF16),
            pltpu.VMEM((len(POOL_WINDOWS), POOL_GROUP, POOL_GROUP), BF16),
            pltpu.VMEM((1, D_SHIFT), F32),
            pltpu.VMEM((2 * LORA, 2 * D_RWKV), BF16),
            pltpu.VMEM((1, D_RWKV), F32),
            pltpu.VMEM((1, D_RWKV), F32),
            pltpu.VMEM((D_MODEL, D_MODEL), BF16),
        ],
        compiler_params=pltpu.CompilerParams(
            dimension_semantics=("arbitrary", "arbitrary"),
            vmem_limit_bytes=VMEM_LIMIT_BYTES),
        name="hybrid_pool_rwkv7_block",
    )(*operands)
```

```python
import math

import jax
import jax.numpy as jnp
from jax import lax
from jax.experimental import pallas as pl
from jax.experimental.pallas import tpu as pltpu

F32 = jnp.float32
BF16 = jnp.bfloat16

D_MODEL = 1024
D_POOL = 512
POOL_WINDOWS = (2, 4, 8, 16)
POOL_GROUP = 128
POOL_HALO = 16
D_RWKV = 512
N_HEADS = 8
HEAD = 64
PAIR = 2 * HEAD
LORA = 64
D_SHIFT = 3 * D_RWKV + 2 * LORA
D_IN = 2 * D_POOL + D_SHIFT + D_RWKV
NORM_EPS = 1e-6
GN_EPS = 64e-5
CHUNK = 64
TILE = 256
N_CHUNKS = TILE // CHUNK
DECAY_SCALE = math.exp(-0.5)
VMEM_LIMIT_BYTES = 56 * 1024 * 1024


def _mm(a, b):
    return jnp.dot(a.astype(BF16), b.astype(BF16), preferred_element_type=F32)


def _split(a, parts):
    out = []
    rem = a
    for _ in range(parts):
        p = rem.astype(BF16)
        out.append(p)
        rem = rem - p.astype(F32)
    return out


def _mm_exact_lhs(a_bf16, b, parts):
    acc = None
    for p in _split(b, parts):
        t = jnp.dot(a_bf16, p, preferred_element_type=F32)
        acc = t if acc is None else acc + t
    return acc


def _sigmoid(t):
    return 0.5 * jnp.tanh(0.5 * t) + 0.5


def _silu(t):
    return t * _sigmoid(t)


def _swap_pair_lanes(a):
    rows = a.shape[0]
    if rows < 8:
        a = jnp.broadcast_to(a[0:1], (8, a.shape[1]))
    out = jnp.concatenate([pltpu.roll(a[:, c:c + PAIR], HEAD, axis=1)
                           for c in range(0, a.shape[1], PAIR)], axis=1)
    return out[:rows]


def _prepare_weights(win_f, poolw_f, mu_f, wup_f, aup_f, gng_f, gnb_f, wout_f,
                     win_ref, poolw_ref, mu_ref, lora_ref, gng_ref, gnb_ref, wout_ref):
    v0 = 2 * D_POOL + 2 * D_RWKV
    g0 = 2 * D_POOL + D_SHIFT
    for c in range(0, D_IN, PAIR):
        w = win_f[0, :, c:c + PAIR]
        if v0 <= c < v0 + D_RWKV or c >= g0:
            w = pltpu.roll(w, HEAD, axis=1)
        win_ref[:, c:c + PAIR] = w.astype(BF16)
    wout_ref[0:D_POOL, :] = wout_f[0, 0:D_POOL, :].astype(BF16)
    for r in range(D_POOL, D_MODEL, PAIR):
        wout_ref[r:r + HEAD, :] = wout_f[0, r + HEAD:r + PAIR, :].astype(BF16)
        wout_ref[r + HEAD:r + PAIR, :] = wout_f[0, r:r + HEAD, :].astype(BF16)
    poolw_ref[...] = poolw_f[0].astype(BF16)
    lora_ref[...] = jnp.zeros_like(lora_ref)
    lora_ref[0:LORA, 0:D_RWKV] = wup_f[0].astype(BF16)
    lora_ref[LORA:2 * LORA, D_RWKV:2 * D_RWKV] = aup_f[0].astype(BF16)
    mu_ref[...] = mu_f[...]
    mu_ref[:, 2 * D_RWKV:3 * D_RWKV] = _swap_pair_lanes(mu_f[:, 2 * D_RWKV:3 * D_RWKV])
    gng_ref[...] = _swap_pair_lanes(gng_f[...])
    gnb_ref[...] = _swap_pair_lanes(gnb_f[...])


def _block_kernel(x_ref, ng_ref, win_f, poolw_f, pscale_ref, mu_f, w0_ref, wup_f,
                  a0_ref, aup_f, kk_ref, ka_ref, rk_ref, gng_f, gnb_f, wout_f, fg_ref,
                  headsum_ref, headswap_ref, cumtri_ref,
                  o_ref,
                  shift_carry, pool_carry, state,
                  at_s, rt_s, bt_s, kt_s, bh_s, kh_s, v_s, ge_s,
                  win_ref, poolw_ref, mu_ref, lora_ref, gng_ref, gnb_ref, wout_ref):
    j = pl.program_id(1)

    @pl.when((pl.program_id(0) == 0) & (j == 0))
    def _():
        _prepare_weights(win_f, poolw_f, mu_f, wup_f, aup_f, gng_f, gnb_f, wout_f,
                         win_ref, poolw_ref, mu_ref, lora_ref, gng_ref, gnb_ref, wout_ref)

    @pl.when(j == 0)
    def _():
        shift_carry[...] = jnp.zeros_like(shift_carry)
        pool_carry[...] = jnp.zeros_like(pool_carry)
        state[...] = jnp.zeros_like(state)

    x = x_ref[0]
    inv = lax.rsqrt(jnp.mean(x * x, axis=-1, keepdims=True) + NORM_EPS)
    h = (x * inv * ng_ref[...]).astype(BF16)

    row = lax.broadcasted_iota(jnp.int32, (TILE, 1), 0)

    z_pool = jnp.dot(h, win_ref[:, 0:2 * D_POOL], preferred_element_type=F32)
    u = z_pool[:, :D_POOL]
    g_a = z_pool[:, D_POOL:]
    ext = jnp.concatenate([pool_carry[...], u], axis=0)
    pool_carry[...] = u[TILE - POOL_HALO:, :]
    pos = (j * TILE + row).astype(F32)
    pooled = []
    s = ext
    for g, w in enumerate(POOL_WINDOWS):
        lo = g * POOL_GROUP
        s = s[:, (POOL_GROUP if g > 0 else 0):]
        s = s + pltpu.roll(s, w // 2, axis=0)
        inv_cnt = 1.0 / jnp.minimum(pos + 1.0, float(w))
        pooled.append(s[POOL_HALO:, :POOL_GROUP] * inv_cnt - u[:, lo:lo + POOL_GROUP])
    mixed = [_mm(pooled[g], poolw_ref[g]) for g in range(len(POOL_WINDOWS))]
    y_a = jnp.concatenate(mixed, axis=1) * pscale_ref[...] * _silu(g_a)

    z_sh = jnp.dot(h, win_ref[:, 2 * D_POOL:2 * D_POOL + D_SHIFT], preferred_element_type=F32)
    prev = pltpu.roll(z_sh, 1, axis=0)
    prev = jnp.where(row == 0, shift_carry[...], prev)
    shift_carry[...] = z_sh[TILE - 1:TILE, :]
    sh = z_sh + mu_ref[...] * (prev - z_sh)

    r = sh[:, 0:D_RWKV]
    k = sh[:, D_RWKV:2 * D_RWKV]
    v = sh[:, 2 * D_RWKV:3 * D_RWKV]
    lr = sh[:, 3 * D_RWKV:]
    lane = lax.broadcasted_iota(jnp.int32, (1, 2 * LORA), 1)
    lr = jnp.where(lane < LORA, jnp.tanh(lr), lr)
    lora = _mm(lr, lora_ref[...])
    ww = w0_ref[...] + lora[:, :D_RWKV]
    a = _sigmoid(a0_ref[...] + lora[:, D_RWKV:])
    logw = -DECAY_SCALE * _sigmoid(ww)

    headsum = headsum_ref[...]
    kkv = k * kk_ref[...]
    ss = _mm(kkv * kkv, headsum)
    kkn = kkv * lax.rsqrt(jnp.maximum(ss, 1e-24))
    k2 = k * (1.0 + (a - 1.0) * ka_ref[...])
    bonus = _mm(r * k2 * rk_ref[...], headswap_ref[...]) * v
    g_b = jnp.dot(h, win_ref[:, 2 * D_POOL + D_SHIFT:], preferred_element_type=F32)
    gate_b = _silu(g_b)

    gcum = _mm_exact_lhs(cumtri_ref[...], logw, 2)
    e_inv = jnp.exp(-gcum)
    kka = kkn * a
    at_s[...] = -kkn * jnp.exp(gcum - logw)
    rt_s[...] = r * jnp.exp(gcum)
    bt_s[...] = kka * e_inv
    kt_s[...] = k2 * e_inv
    v_s[...] = v
    for c in range(N_CHUNKS):
        rows = slice(c * CHUNK, (c + 1) * CHUNK)
        gam_c = jnp.exp(gcum[rows.stop - 1:rows.stop, :])
        ge_s[c:c + 1, :] = gam_c
        e_end = e_inv[rows] * gam_c
        bh_s[rows, :] = kka[rows] * e_end
        kh_s[rows, :] = k2[rows] * e_end

    ri = lax.broadcasted_iota(jnp.int32, (CHUNK, PAIR), 0)
    ci = lax.broadcasted_iota(jnp.int32, (CHUNK, PAIR), 1)
    cm = ci % CHUNK
    left = ci < CHUNK
    tri_k = jnp.concatenate([(ri > cm) & ~left, (ri >= cm) & ~left], axis=0)
    strict_b = (ri > cm) & left
    incl_b = (ri >= cm) & left
    eye_r = jnp.where(ci == ri + CHUNK, 1.0, 0.0)
    lane_half = lax.broadcasted_iota(jnp.int32, (1, PAIR), 1) // HEAD
    key_mask = [lane_half == e for e in range(2)]
    eye_key = [ci == ri + e * HEAD for e in range(2)]
    zeros_blk = jnp.zeros((HEAD, PAIR), F32)
    keep_right = jnp.where(lane_half == 1, 1.0, 0.0)
    n_doubling = int(math.log2(CHUNK))
    heads = [(p, e) for p in range(N_HEADS // 2) for e in range(2)]
    chains = [(c, p, e) for c in range(N_CHUNKS) for p, e in heads]
    blk = lambda ref, c, p: ref[c * CHUNK:(c + 1) * CHUNK, p * PAIR:(p + 1) * PAIR]

    aall, vm, z, ry, tx, mgk = [], [], [], [], [], []
    for c, p, e in chains:
        if e == 0:
            lq = jnp.concatenate([blk(at_s, c, p), blk(rt_s, c, p)], axis=0)
            rk_t = jnp.concatenate([blk(bt_s, c, p), blk(kt_s, c, p)], axis=0).T
            rk_pair = jnp.concatenate(
                [jnp.concatenate([rk_t[:HEAD], zeros_blk], axis=1),
                 jnp.concatenate([zeros_blk, rk_t[HEAD:]], axis=1)], axis=0)
            scores = _mm(lq, rk_pair)
        aall.append(scores[:, e * PAIR:(e + 1) * PAIR])
        vm.append(jnp.where(key_mask[e], 0.0, blk(v_s, c, p)))
    for i in range(len(chains)):
        tx.append(jnp.where(strict_b, aall[i][:CHUNK], eye_r))
    per_level = -(-len(chains) // n_doubling)
    for level in range(n_doubling):
        for i in range(len(chains)):
            tx[i] = tx[i] * keep_right + _mm(tx[i][:, :CHUNK], tx[i])
        for i, (c, p, e) in list(enumerate(chains))[level * per_level:(level + 1) * per_level]:
            a_k = jnp.where(tri_k, aall[i], 0.0)
            av = _mm(a_k, jnp.concatenate([vm[i], vm[i]], axis=0))
            z.append(jnp.where(key_mask[e], blk(at_s, c, p), av[:CHUNK]))
            ry.append(jnp.where(key_mask[e], blk(rt_s, c, p), av[CHUNK:]))
    for i in range(len(chains)):
        z[i] = _mm(tx[i], jnp.concatenate([z[i], z[i]], axis=0))
    for i, (c, p, e) in enumerate(chains):
        a_rb = jnp.where(incl_b, aall[i][CHUNK:], 0.0)
        if e == 0:
            bk_t = jnp.concatenate([blk(bh_s, c, p), blk(kh_s, c, p)], axis=0).T
        both = _mm(jnp.concatenate([a_rb, bk_t[e * HEAD:(e + 1) * HEAD]], axis=0),
                   jnp.concatenate([z[i], vm[i]], axis=0))
        ry[i] = ry[i] + both[:CHUNK]
        gam_c = ge_s[c:c + 1, p * PAIR:(p + 1) * PAIR]
        mgk.append(both[CHUNK:] + jnp.where(eye_key[e], gam_c, 0.0))
    y_blocks = []
    for c in range(N_CHUNKS):
        ys = []
        for hd, (p, e) in enumerate(heads):
            i = c * N_HEADS + hd
            sp = state[hd]
            lhs = jnp.concatenate([ry[i], mgk[i]], axis=0)
            if e == 0:
                res = _mm(lhs[:, :HEAD], sp)
            else:
                res = _mm(lhs, jnp.concatenate([zeros_blk, sp], axis=0))
            ys.append(res[:CHUNK] + ry[i])
            state[hd] = res[CHUNK:] + jnp.where(key_mask[e], 0.0, mgk[i])
        y_blocks.append(jnp.concatenate(
            [jnp.where(key_mask[0], ys[2 * p + 1], ys[2 * p]) for p in range(N_HEADS // 2)],
            axis=1))
    y = jnp.concatenate(y_blocks, axis=0)

    mean = _mm(y, headsum) * (1.0 / HEAD)
    d = y - mean
    var = _mm(d * d, headsum) * (1.0 / HEAD)
    y = d * lax.rsqrt(var + GN_EPS) * gng_ref[...] + gnb_ref[...]
    y_b = (y + bonus) * gate_b

    yy = jnp.concatenate([y_a, y_b], axis=1).astype(BF16)
    xo = x + jnp.dot(yy, wout_ref[...], preferred_element_type=F32)
    inv_o = lax.rsqrt(jnp.mean(xo * xo, axis=-1, keepdims=True) + NORM_EPS)
    o_ref[0] = xo * inv_o * fg_ref[...]


def _full(shape, single_buffer=False):
    index_map = lambda b, j: (0,) * len(shape)
    if single_buffer:
        return pl.BlockSpec(shape, index_map, pipeline_mode=pl.Buffered(1))
    return pl.BlockSpec(shape, index_map)


def kernel(x, norm_gain, w_in, pool_w, pool_scale, shift_mu, w0, w_up, a0, a_up, k_k, k_a, r_k,
           gn_gain, gn_bias, w_out, final_gain):
    batch, seq, d_model = x.shape
    assert norm_gain.shape[0] == 1 and d_model == D_MODEL and seq % TILE == 0
    assert w_in.shape == (1, D_MODEL, D_IN)

    hid = jnp.arange(D_RWKV) // HEAD
    headsum = (hid[:, None] == hid[None, :]).astype(BF16)
    headsum_swap = (hid[:, None] == (hid[None, :] ^ 1)).astype(BF16)
    t = jnp.arange(TILE)
    same_chunk = (t[:, None] // CHUNK) == (t[None, :] // CHUNK)
    cumtri = (same_chunk & (t[:, None] >= t[None, :])).astype(BF16)

    operands = (
        x, norm_gain, w_in, pool_w, pool_scale, shift_mu, w0, w_up, a0, a_up, k_k, k_a,
        r_k.reshape(1, D_RWKV), gn_gain, gn_bias, w_out, final_gain.reshape(1, D_MODEL),
        headsum, headsum_swap, cumtri,
    )
    big = {2, 15}
    in_specs = [pl.BlockSpec((1, TILE, D_MODEL), lambda b, j: (b, j, 0))]
    in_specs += [_full(op.shape, i in big) for i, op in enumerate(operands) if i > 0]
    seq_buf = lambda: pltpu.VMEM((TILE, D_RWKV), F32)
    return pl.pallas_call(
        _block_kernel,
        grid=(batch, seq // TILE),
        in_specs=in_specs,
        out_specs=pl.BlockSpec((1, TILE, D_MODEL), lambda b, j: (b, j, 0)),
        out_shape=jax.ShapeDtypeStruct(x.shape, x.dtype),
        scratch_shapes=[
            pltpu.VMEM((1, D_SHIFT), F32),
            pltpu.VMEM((POOL_HALO, D_POOL), F32),
            pltpu.VMEM((N_HEADS, HEAD, PAIR), F32),
        ] + [seq_buf() for _ in range(7)] + [
            pltpu.VMEM((8, D_RWKV), F32),
            pltpu.VMEM((D_MODEL, D_IN), BF16),
            pltpu.VMEM((len(POOL_WINDOWS), POOL_GROUP, POOL_GROUP), BF16),
            pltpu.VMEM((1, D_SHIFT), F32),
            pltpu.VMEM((2 * LORA, 2 * D_RWKV), BF16),
            pltpu.VMEM((1, D_RWKV), F32),
            pltpu.VMEM((1, D_RWKV), F32),
            pltpu.VMEM((D_MODEL, D_MODEL), BF16),
        ],
        compiler_params=pltpu.CompilerParams(
            dimension_semantics=("arbitrary", "arbitrary"),
            vmem_limit_bytes=VMEM_LIMIT_BYTES),
        name="hybrid_pool_rwkv7_block",
    )(*operands)
```

```python
import math

import jax
import jax.numpy as jnp
from jax import lax
from jax.experimental import pallas as pl
from jax.experimental.pallas import tpu as pltpu

F32 = jnp.float32
BF16 = jnp.bfloat16

D_MODEL = 1024
D_POOL = 512
POOL_WINDOWS = (2, 4, 8, 16)
POOL_GROUP = 128
POOL_HALO = 16
D_RWKV = 512
N_HEADS = 8
HEAD = 64
PAIR = 2 * HEAD
LORA = 64
D_SHIFT = 3 * D_RWKV + 2 * LORA
D_IN = 2 * D_POOL + D_SHIFT + D_RWKV
NORM_EPS = 1e-6
GN_EPS = 64e-5
CHUNK = 64
TILE = 256
STEP_TILES = 4
N_CHUNKS = TILE // CHUNK
DECAY_SCALE = math.exp(-0.5)
VMEM_LIMIT_BYTES = 56 * 1024 * 1024


def _mm(a, b):
    return jnp.dot(a.astype(BF16), b.astype(BF16), preferred_element_type=F32)


def _split(a, parts):
    out = []
    rem = a
    for _ in range(parts):
        p = rem.astype(BF16)
        out.append(p)
        rem = rem - p.astype(F32)
    return out


def _mm_exact_lhs(a_bf16, b, parts):
    acc = None
    for p in _split(b, parts):
        t = jnp.dot(a_bf16, p, preferred_element_type=F32)
        acc = t if acc is None else acc + t
    return acc


def _sigmoid(t):
    return 0.5 * jnp.tanh(0.5 * t) + 0.5


def _silu(t):
    return t * _sigmoid(t)


def _swap_pair_lanes(a):
    rows = a.shape[0]
    if rows < 8:
        a = jnp.broadcast_to(a[0:1], (8, a.shape[1]))
    out = jnp.concatenate([pltpu.roll(a[:, c:c + PAIR], HEAD, axis=1)
                           for c in range(0, a.shape[1], PAIR)], axis=1)
    return out[:rows]


def _prepare_weights(win_f, poolw_f, mu_f, wup_f, aup_f, gng_f, gnb_f, wout_f,
                     win_ref, poolw_ref, mu_ref, lora_ref, gng_ref, gnb_ref, wout_ref):
    v0 = 2 * D_POOL + 2 * D_RWKV
    g0 = 2 * D_POOL + D_SHIFT
    for c in range(0, D_IN, PAIR):
        w = win_f[0, :, c:c + PAIR]
        if v0 <= c < v0 + D_RWKV or c >= g0:
            w = pltpu.roll(w, HEAD, axis=1)
        win_ref[:, c:c + PAIR] = w.astype(BF16)
    wout_ref[0:D_POOL, :] = wout_f[0, 0:D_POOL, :].astype(BF16)
    for r in range(D_POOL, D_MODEL, PAIR):
        wout_ref[r:r + HEAD, :] = wout_f[0, r + HEAD:r + PAIR, :].astype(BF16)
        wout_ref[r + HEAD:r + PAIR, :] = wout_f[0, r:r + HEAD, :].astype(BF16)
    poolw_ref[...] = poolw_f[0].astype(BF16)
    lora_ref[...] = jnp.zeros_like(lora_ref)
    lora_ref[0:LORA, 0:D_RWKV] = wup_f[0].astype(BF16)
    lora_ref[LORA:2 * LORA, D_RWKV:2 * D_RWKV] = aup_f[0].astype(BF16)
    mu_ref[...] = mu_f[...]
    mu_ref[:, 2 * D_RWKV:3 * D_RWKV] = _swap_pair_lanes(mu_f[:, 2 * D_RWKV:3 * D_RWKV])
    gng_ref[...] = _swap_pair_lanes(gng_f[...])
    gnb_ref[...] = _swap_pair_lanes(gnb_f[...])


def _tile(j, tile_rows, x_ref, ng_ref, pscale_ref, w0_ref, a0_ref, kk_ref, ka_ref, rk_ref, fg_ref,
          headsum_ref, headswap_ref, cumtri_ref, o_ref, shift_carry, pool_carry, state,
          at_s, rt_s, bt_s, kt_s, bh_s, kh_s, v_s, ge_s,
          win_ref, poolw_ref, mu_ref, lora_ref, gng_ref, gnb_ref, wout_ref):
    @pl.when(j == 0)
    def _():
        shift_carry[...] = jnp.zeros_like(shift_carry)
        pool_carry[...] = jnp.zeros_like(pool_carry)
        state[...] = jnp.zeros_like(state)

    x = x_ref[0, tile_rows, :]
    inv = lax.rsqrt(jnp.mean(x * x, axis=-1, keepdims=True) + NORM_EPS)
    h = (x * inv * ng_ref[...]).astype(BF16)

    row = lax.broadcasted_iota(jnp.int32, (TILE, 1), 0)

    z_pool = jnp.dot(h, win_ref[:, 0:2 * D_POOL], preferred_element_type=F32)
    u = z_pool[:, :D_POOL]
    g_a = z_pool[:, D_POOL:]
    ext = jnp.concatenate([pool_carry[...], u], axis=0)
    pool_carry[...] = u[TILE - POOL_HALO:, :]
    pos = (j * TILE + row).astype(F32)
    pooled = []
    s = ext
    for g, w in enumerate(POOL_WINDOWS):
        lo = g * POOL_GROUP
        s = s[:, (POOL_GROUP if g > 0 else 0):]
        s = s + pltpu.roll(s, w // 2, axis=0)
        inv_cnt = 1.0 / jnp.minimum(pos + 1.0, float(w))
        pooled.append(s[POOL_HALO:, :POOL_GROUP] * inv_cnt - u[:, lo:lo + POOL_GROUP])
    mixed = [_mm(pooled[g], poolw_ref[g]) for g in range(len(POOL_WINDOWS))]
    y_a = jnp.concatenate(mixed, axis=1) * pscale_ref[...] * _silu(g_a)

    z_sh = jnp.dot(h, win_ref[:, 2 * D_POOL:2 * D_POOL + D_SHIFT], preferred_element_type=F32)
    prev = pltpu.roll(z_sh, 1, axis=0)
    head_rows = jnp.where(row[:8] == 0, shift_carry[...], prev[:8])
    prev = jnp.concatenate([head_rows, prev[8:]], axis=0)
    shift_carry[...] = z_sh[TILE - 1:TILE, :]
    sh = z_sh + mu_ref[...] * (prev - z_sh)

    r = sh[:, 0:D_RWKV]
    k = sh[:, D_RWKV:2 * D_RWKV]
    v = sh[:, 2 * D_RWKV:3 * D_RWKV]
    lr = sh[:, 3 * D_RWKV:]
    lane = lax.broadcasted_iota(jnp.int32, (1, 2 * LORA), 1)
    lr = jnp.where(lane < LORA, jnp.tanh(lr), lr)
    lora = _mm(lr, lora_ref[...])
    ww = w0_ref[...] + lora[:, :D_RWKV]
    a = _sigmoid(a0_ref[...] + lora[:, D_RWKV:])
    logw = -DECAY_SCALE * _sigmoid(ww)

    headsum = headsum_ref[...]
    kkv = k * kk_ref[...]
    ss = _mm(kkv * kkv, headsum)
    kkn = kkv * lax.rsqrt(jnp.maximum(ss, 1e-24))
    k2 = k * (1.0 + (a - 1.0) * ka_ref[...])
    bonus = _mm(r * k2 * rk_ref[...], headswap_ref[...]) * v
    g_b = jnp.dot(h, win_ref[:, 2 * D_POOL + D_SHIFT:], preferred_element_type=F32)
    gate_b = _silu(g_b)

    gcum = _mm_exact_lhs(cumtri_ref[...], logw, 2)
    e_inv = jnp.exp(-gcum)
    kka = kkn * a
    at_s[...] = -kkn * jnp.exp(gcum - logw)
    rt_s[...] = r * jnp.exp(gcum)
    bt_s[...] = kka * e_inv
    kt_s[...] = k2 * e_inv
    v_s[...] = v
    for c in range(N_CHUNKS):
        rows = slice(c * CHUNK, (c + 1) * CHUNK)
        gam_c = jnp.exp(gcum[rows.stop - 1:rows.stop, :])
        ge_s[c:c + 1, :] = gam_c
        e_end = e_inv[rows] * gam_c
        bh_s[rows, :] = kka[rows] * e_end
        kh_s[rows, :] = k2[rows] * e_end

    ri = lax.broadcasted_iota(jnp.int32, (CHUNK, PAIR), 0)
    ci = lax.broadcasted_iota(jnp.int32, (CHUNK, PAIR), 1)
    cm = ci % CHUNK
    left = ci < CHUNK
    strict_k = (ri > cm) & ~left
    strict_b = (ri > cm) & left
    incl = ri >= cm
    eye_r = jnp.where(ci == ri + CHUNK, 1.0, 0.0)
    lane_half = lax.broadcasted_iota(jnp.int32, (1, PAIR), 1) // HEAD
    key_mask = [lane_half == e for e in range(2)]
    eye_key = [ci == ri + e * HEAD for e in range(2)]
    zeros_blk = jnp.zeros((HEAD, PAIR), F32)
    keep_right = jnp.where(lane_half == 1, 1.0, 0.0)
    n_doubling = int(math.log2(CHUNK))
    heads = [(p, e) for p in range(N_HEADS // 2) for e in range(2)]
    chains = [(c, p, e) for c in range(N_CHUNKS) for p, e in heads]
    blk = lambda ref, c, p: ref[c * CHUNK:(c + 1) * CHUNK, p * PAIR:(p + 1) * PAIR]

    aall, vm, z, ry, tx, mgk = [], [], [], [], [], []
    for c, p, e in chains:
        if e == 0:
            lq = jnp.concatenate([blk(at_s, c, p), blk(rt_s, c, p)], axis=0)
            rk_t = jnp.concatenate([blk(bt_s, c, p), blk(kt_s, c, p)], axis=0).T
            rk_pair = jnp.concatenate(
                [jnp.concatenate([rk_t[:HEAD], zeros_blk], axis=1),
                 jnp.concatenate([zeros_blk, rk_t[HEAD:]], axis=1)], axis=0)
            scores = _mm(lq, rk_pair)
        aall.append(scores[:, e * PAIR:(e + 1) * PAIR])
        vm.append(jnp.where(key_mask[e], 0.0, blk(v_s, c, p)))
    for i in range(len(chains)):
        tx.append(jnp.where(strict_b, aall[i][:CHUNK], eye_r))
    per_level = -(-len(chains) // n_doubling)
    for level in range(n_doubling):
        for i in range(len(chains)):
            tx[i] = tx[i] * keep_right + _mm(tx[i][:, :CHUNK], tx[i])
        for i, (c, p, e) in list(enumerate(chains))[level * per_level:(level + 1) * per_level]:
            a_ak = jnp.where(strict_k, aall[i][:CHUNK], 0.0)
            av = _mm(a_ak, jnp.concatenate([vm[i], vm[i]], axis=0))
            z.append(jnp.where(key_mask[e], blk(at_s, c, p), av))
    for i in range(len(chains)):
        z[i] = _mm(tx[i], jnp.concatenate([z[i], z[i]], axis=0))
    for i, (c, p, e) in enumerate(chains):
        a_r = jnp.where(incl, aall[i][CHUNK:], 0.0)
        if e == 0:
            bk_t = jnp.concatenate([blk(bh_s, c, p), blk(kh_s, c, p)], axis=0).T
        both = _mm(jnp.concatenate([a_r, bk_t[e * HEAD:(e + 1) * HEAD]], axis=0),
                   jnp.concatenate([z[i], vm[i]], axis=0))
        ry.append(jnp.where(key_mask[e], blk(rt_s, c, p), 0.0) + both[:CHUNK])
        gam_c = ge_s[c:c + 1, p * PAIR:(p + 1) * PAIR]
        mgk.append(both[CHUNK:] + jnp.where(eye_key[e], gam_c, 0.0))
    y_blocks = []
    for c in range(N_CHUNKS):
        ys = []
        for hd, (p, e) in enumerate(heads):
            i = c * N_HEADS + hd
            sp = state[hd]
            lhs = jnp.concatenate([ry[i], mgk[i]], axis=0)
            if e == 0:
                res = _mm(lhs[:, :HEAD], sp)
            else:
                res = _mm(lhs, jnp.concatenate([zeros_blk, sp], axis=0))
            ys.append(res[:CHUNK] + ry[i])
            state[hd] = res[CHUNK:] + jnp.where(key_mask[e], 0.0, mgk[i])
        y_blocks.append(jnp.concatenate(
            [jnp.where(key_mask[0], ys[2 * p + 1], ys[2 * p]) for p in range(N_HEADS // 2)],
            axis=1))
    y = jnp.concatenate(y_blocks, axis=0)

    mean = _mm(y, headsum) * (1.0 / HEAD)
    d = y - mean
    var = _mm(d * d, headsum) * (1.0 / HEAD)
    y = d * lax.rsqrt(var + GN_EPS) * gng_ref[...] + gnb_ref[...]
    y_b = (y + bonus) * gate_b

    yy = jnp.concatenate([y_a, y_b], axis=1).astype(BF16)
    xo = x + jnp.dot(yy, wout_ref[...], preferred_element_type=F32)
    inv_o = lax.rsqrt(jnp.mean(xo * xo, axis=-1, keepdims=True) + NORM_EPS)
    o_ref[0, tile_rows, :] = xo * inv_o * fg_ref[...]


def _block_kernel(x_ref, ng_ref, win_f, poolw_f, pscale_ref, mu_f, w0_ref, wup_f,
                  a0_ref, aup_f, kk_ref, ka_ref, rk_ref, gng_f, gnb_f, wout_f, fg_ref,
                  headsum_ref, headswap_ref, cumtri_ref,
                  o_ref,
                  shift_carry, pool_carry, state,
                  at_s, rt_s, bt_s, kt_s, bh_s, kh_s, v_s, ge_s,
                  win_ref, poolw_ref, mu_ref, lora_ref, gng_ref, gnb_ref, wout_ref):
    step = pl.program_id(1)

    @pl.when((pl.program_id(0) == 0) & (step == 0))
    def _():
        _prepare_weights(win_f, poolw_f, mu_f, wup_f, aup_f, gng_f, gnb_f, wout_f,
                         win_ref, poolw_ref, mu_ref, lora_ref, gng_ref, gnb_ref, wout_ref)

    def body(i, carry):
        _tile(step * STEP_TILES + i, pl.ds(pl.multiple_of(i * TILE, TILE), TILE),
              x_ref, ng_ref, pscale_ref, w0_ref, a0_ref, kk_ref, ka_ref, rk_ref, fg_ref,
              headsum_ref, headswap_ref, cumtri_ref, o_ref, shift_carry, pool_carry, state,
              at_s, rt_s, bt_s, kt_s, bh_s, kh_s, v_s, ge_s,
              win_ref, poolw_ref, mu_ref, lora_ref, gng_ref, gnb_ref, wout_ref)
        return carry

    lax.fori_loop(0, STEP_TILES, body, 0)


def _full(shape, single_buffer=False):
    index_map = lambda b, j: (0,) * len(shape)
    if single_buffer:
        return pl.BlockSpec(shape, index_map, pipeline_mode=pl.Buffered(1))
    return pl.BlockSpec(shape, index_map)


def kernel(x, norm_gain, w_in, pool_w, pool_scale, shift_mu, w0, w_up, a0, a_up, k_k, k_a, r_k,
           gn_gain, gn_bias, w_out, final_gain):
    batch, seq, d_model = x.shape
    step_rows = STEP_TILES * TILE
    assert norm_gain.shape[0] == 1 and d_model == D_MODEL and seq % step_rows == 0
    assert w_in.shape == (1, D_MODEL, D_IN)

    hid = jnp.arange(D_RWKV) // HEAD
    headsum = (hid[:, None] == hid[None, :]).astype(BF16)
    headsum_swap = (hid[:, None] == (hid[None, :] ^ 1)).astype(BF16)
    t = jnp.arange(TILE)
    same_chunk = (t[:, None] // CHUNK) == (t[None, :] // CHUNK)
    cumtri = (same_chunk & (t[:, None] >= t[None, :])).astype(BF16)

    operands = (
        x, norm_gain, w_in, pool_w, pool_scale, shift_mu, w0, w_up, a0, a_up, k_k, k_a,
        r_k.reshape(1, D_RWKV), gn_gain, gn_bias, w_out, final_gain.reshape(1, D_MODEL),
        headsum, headsum_swap, cumtri,
    )
    big = {2, 15}
    in_specs = [pl.BlockSpec((1, step_rows, D_MODEL), lambda b, j: (b, j, 0))]
    in_specs += [_full(op.shape, i in big) for i, op in enumerate(operands) if i > 0]
    seq_buf = lambda: pltpu.VMEM((TILE, D_RWKV), F32)
    return pl.pallas_call(
        _block_kernel,
        grid=(batch, seq // step_rows),
        in_specs=in_specs,
        out_specs=pl.BlockSpec((1, step_rows, D_MODEL), lambda b, j: (b, j, 0)),
        out_shape=jax.ShapeDtypeStruct(x.shape, x.dtype),
        scratch_shapes=[
            pltpu.VMEM((1, D_SHIFT), F32),
            pltpu.VMEM((POOL_HALO, D_POOL), F32),
            pltpu.VMEM((N_HEADS, HEAD, PAIR), F32),
        ] + [seq_buf() for _ in range(7)] + [
            pltpu.VMEM((8, D_RWKV), F32),
            pltpu.VMEM((D_MODEL, D_IN), BF16),
            pltpu.VMEM((len(POOL_WINDOWS), POOL_GROUP, POOL_GROUP), BF16),
            pltpu.VMEM((1, D_SHIFT), F32),
            pltpu.VMEM((2 * LORA, 2 * D_RWKV), BF16),
            pltpu.VMEM((1, D_RWKV), F32),
            pltpu.VMEM((1, D_RWKV), F32),
            pltpu.VMEM((D_MODEL, D_MODEL), BF16),
        ],
        compiler_params=pltpu.CompilerParams(
            dimension_semantics=("arbitrary", "arbitrary"),
            vmem_limit_bytes=VMEM_LIMIT_BYTES),
        name="hybrid_pool_rwkv7_block",
    )(*operands)
```

```python
import math

import jax
import jax.numpy as jnp
from jax import lax
from jax.experimental import pallas as pl
from jax.experimental.pallas import tpu as pltpu

F32 = jnp.float32
BF16 = jnp.bfloat16

D_MODEL = 1024
D_POOL = 512
POOL_WINDOWS = (2, 4, 8, 16)
POOL_GROUP = 128
POOL_HALO = 16
D_RWKV = 512
N_HEADS = 8
HEAD = 64
PAIR = 2 * HEAD
LORA = 64
D_SHIFT = 3 * D_RWKV + 2 * LORA
D_IN = 2 * D_POOL + D_SHIFT + D_RWKV
SHIFT_PAD = 4096
NORM_EPS = 1e-6
GN_EPS = 64e-5
CHUNK = 64
TILE = 256
STEP_TILES = 4
N_CHUNKS = TILE // CHUNK
DECAY_SCALE = math.exp(-0.5)
VMEM_LIMIT_BYTES = 56 * 1024 * 1024


def _mm(a, b):
    return jnp.dot(a.astype(BF16), b.astype(BF16), preferred_element_type=F32)


def _split(a, parts):
    out = []
    rem = a
    for _ in range(parts):
        p = rem.astype(BF16)
        out.append(p)
        rem = rem - p.astype(F32)
    return out


def _mm_exact_lhs(a_bf16, b, parts):
    acc = None
    for p in _split(b, parts):
        t = jnp.dot(a_bf16, p, preferred_element_type=F32)
        acc = t if acc is None else acc + t
    return acc


def _sigmoid(t):
    return 0.5 * jnp.tanh(0.5 * t) + 0.5


def _silu(t):
    return t * _sigmoid(t)


def _swap_pair_lanes(a):
    rows = a.shape[0]
    if rows < 8:
        a = jnp.broadcast_to(a[0:1], (8, a.shape[1]))
    out = jnp.concatenate([pltpu.roll(a[:, c:c + PAIR], HEAD, axis=1)
                           for c in range(0, a.shape[1], PAIR)], axis=1)
    return out[:rows]


def _prepare_weights(win_f, poolw_f, mu_f, wup_f, aup_f, gng_f, gnb_f, wout_f,
                     win_ref, poolw_ref, mu_ref, lora_ref, gng_ref, gnb_ref, wout_ref):
    v0 = 2 * D_POOL + 2 * D_RWKV
    g0 = 2 * D_POOL + D_SHIFT
    for c in range(0, D_IN, PAIR):
        w = win_f[0, :, c:c + PAIR]
        if v0 <= c < v0 + D_RWKV or c >= g0:
            w = pltpu.roll(w, HEAD, axis=1)
        win_ref[:, c:c + PAIR] = w.astype(BF16)
    wout_ref[0:D_POOL, :] = wout_f[0, 0:D_POOL, :].astype(BF16)
    for r in range(D_POOL, D_MODEL, PAIR):
        wout_ref[r:r + HEAD, :] = wout_f[0, r + HEAD:r + PAIR, :].astype(BF16)
        wout_ref[r + HEAD:r + PAIR, :] = wout_f[0, r:r + HEAD, :].astype(BF16)
    poolw_ref[...] = poolw_f[0].astype(BF16)
    lora_ref[...] = jnp.zeros_like(lora_ref)
    lora_ref[0:LORA, 0:D_RWKV] = wup_f[0].astype(BF16)
    lora_ref[LORA:2 * LORA, D_RWKV:2 * D_RWKV] = aup_f[0].astype(BF16)
    for c in range(0, D_SHIFT, PAIR):
        mu_row = mu_f[c // PAIR:c // PAIR + 1, :]
        if 2 * D_RWKV <= c < 3 * D_RWKV:
            mu_row = _swap_pair_lanes(mu_row)
        mu_ref[:, c:c + PAIR] = mu_row
    gng_ref[0:1, :] = _swap_pair_lanes(gng_f[...])
    gnb_ref[0:1, :] = _swap_pair_lanes(gnb_f[...])


def _tile(j, tile_rows, x_ref, ng_ref, pscale_ref, w0_ref, a0_ref, kk_ref, ka_ref, rk_ref, fg_ref,
          headsum_ref, headswap_ref, cumtri_ref, o_ref, shift_carry, pool_carry, state,
          at_s, rt_s, bt_s, kt_s, bh_s, kh_s, v_s, ge_s,
          win_ref, poolw_ref, mu_ref, lora_ref, gng_ref, gnb_ref, wout_ref):
    @pl.when(j == 0)
    def _():
        shift_carry[...] = jnp.zeros_like(shift_carry)
        pool_carry[...] = jnp.zeros_like(pool_carry)
        state[...] = jnp.zeros_like(state)

    x = x_ref[0, tile_rows, :]
    inv = lax.rsqrt(jnp.mean(x * x, axis=-1, keepdims=True) + NORM_EPS)
    h = (x * inv * ng_ref[...]).astype(BF16)

    row = lax.broadcasted_iota(jnp.int32, (TILE, 1), 0)

    z_pool = jnp.dot(h, win_ref[:, 0:2 * D_POOL], preferred_element_type=F32)
    u = z_pool[:, :D_POOL]
    g_a = z_pool[:, D_POOL:]
    ext = jnp.concatenate([pool_carry[...], u], axis=0)
    pool_carry[...] = u[TILE - POOL_HALO:, :]
    pos = (j * TILE + row).astype(F32)
    pooled = []
    s = ext
    for g, w in enumerate(POOL_WINDOWS):
        lo = g * POOL_GROUP
        s = s[:, (POOL_GROUP if g > 0 else 0):]
        s = s + pltpu.roll(s, w // 2, axis=0)
        inv_cnt = 1.0 / jnp.minimum(pos + 1.0, float(w))
        pooled.append(s[POOL_HALO:, :POOL_GROUP] * inv_cnt - u[:, lo:lo + POOL_GROUP])
    mixed = [_mm(pooled[g], poolw_ref[g]) for g in range(len(POOL_WINDOWS))]
    y_a = jnp.concatenate(mixed, axis=1) * pscale_ref[...] * _silu(g_a)

    z_sh = jnp.dot(h, win_ref[:, 2 * D_POOL:2 * D_POOL + D_SHIFT], preferred_element_type=F32)
    prev = pltpu.roll(z_sh, 1, axis=0)
    head_rows = jnp.where(row[:8] == 0, shift_carry[:, :D_SHIFT], prev[:8])
    prev = jnp.concatenate([head_rows, prev[8:]], axis=0)
    shift_carry[:, :D_SHIFT] = z_sh[TILE - 1:TILE, :]
    sh = z_sh + mu_ref[:, :D_SHIFT] * (prev - z_sh)

    r = sh[:, 0:D_RWKV]
    k = sh[:, D_RWKV:2 * D_RWKV]
    v = sh[:, 2 * D_RWKV:3 * D_RWKV]
    lr = sh[:, 3 * D_RWKV:]
    lane = lax.broadcasted_iota(jnp.int32, (1, 2 * LORA), 1)
    lr = jnp.where(lane < LORA, jnp.tanh(lr), lr)
    lora = _mm(lr, lora_ref[...])
    ww = w0_ref[...] + lora[:, :D_RWKV]
    a = _sigmoid(a0_ref[...] + lora[:, D_RWKV:])
    logw = -DECAY_SCALE * _sigmoid(ww)

    headsum = headsum_ref[...]
    kkv = k * kk_ref[...]
    ss = _mm(kkv * kkv, headsum)
    kkn = kkv * lax.rsqrt(jnp.maximum(ss, 1e-24))
    k2 = k * (1.0 + (a - 1.0) * ka_ref[...])
    bonus = _mm(r * k2 * rk_ref[...], headswap_ref[...]) * v
    g_b = jnp.dot(h, win_ref[:, 2 * D_POOL + D_SHIFT:], preferred_element_type=F32)
    gate_b = _silu(g_b)

    gcum = _mm_exact_lhs(cumtri_ref[...], logw, 2)
    e_inv = jnp.exp(-gcum)
    kka = kkn * a
    at_s[...] = -kkn * jnp.exp(gcum - logw)
    rt_s[...] = r * jnp.exp(gcum)
    bt_s[...] = kka * e_inv
    kt_s[...] = k2 * e_inv
    v_s[...] = v
    for c in range(N_CHUNKS):
        rows = slice(c * CHUNK, (c + 1) * CHUNK)
        gam_c = jnp.exp(gcum[rows.stop - 1:rows.stop, :])
        ge_s[c:c + 1, :] = gam_c
        e_end = e_inv[rows] * gam_c
        bh_s[rows, :] = kka[rows] * e_end
        kh_s[rows, :] = k2[rows] * e_end

    ri = lax.broadcasted_iota(jnp.int32, (CHUNK, PAIR), 0)
    ci = lax.broadcasted_iota(jnp.int32, (CHUNK, PAIR), 1)
    cm = ci % CHUNK
    left = ci < CHUNK
    strict_k = (ri > cm) & ~left
    strict_b = (ri > cm) & left
    incl = ri >= cm
    eye_r = jnp.where(ci == ri + CHUNK, 1.0, 0.0)
    lane_half = lax.broadcasted_iota(jnp.int32, (1, PAIR), 1) // HEAD
    key_mask = [lane_half == e for e in range(2)]
    eye_key = [ci == ri + e * HEAD for e in range(2)]
    zeros_blk = jnp.zeros((HEAD, PAIR), F32)
    keep_right = jnp.where(lane_half == 1, 1.0, 0.0)
    n_doubling = int(math.log2(CHUNK))
    heads = [(p, e) for p in range(N_HEADS // 2) for e in range(2)]
    chains = [(c, p, e) for c in range(N_CHUNKS) for p, e in heads]
    blk = lambda ref, c, p: ref[c * CHUNK:(c + 1) * CHUNK, p * PAIR:(p + 1) * PAIR]

    aall, vm, z, ry, tx, mgk = [], [], [], [], [], []
    for c, p, e in chains:
        if e == 0:
            lq = jnp.concatenate([blk(at_s, c, p), blk(rt_s, c, p)], axis=0)
            rk_t = jnp.concatenate([blk(bt_s, c, p), blk(kt_s, c, p)], axis=0).T
            rk_pair = jnp.concatenate(
                [jnp.concatenate([rk_t[:HEAD], zeros_blk], axis=1),
                 jnp.concatenate([zeros_blk, rk_t[HEAD:]], axis=1)], axis=0)
            scores = _mm(lq, rk_pair)
        aall.append(scores[:, e * PAIR:(e + 1) * PAIR])
        vm.append(jnp.where(key_mask[e], 0.0, blk(v_s, c, p)))
    for i in range(len(chains)):
        tx.append(jnp.where(strict_b, aall[i][:CHUNK], eye_r))
    per_level = -(-len(chains) // n_doubling)
    for level in range(n_doubling):
        for i in range(len(chains)):
            tx[i] = tx[i] * keep_right + _mm(tx[i][:, :CHUNK], tx[i])
        for i, (c, p, e) in list(enumerate(chains))[level * per_level:(level + 1) * per_level]:
            a_ak = jnp.where(strict_k, aall[i][:CHUNK], 0.0)
            av = _mm(a_ak, jnp.concatenate([vm[i], vm[i]], axis=0))
            z.append(jnp.where(key_mask[e], blk(at_s, c, p), av))
    for i in range(len(chains)):
        z[i] = _mm(tx[i], jnp.concatenate([z[i], z[i]], axis=0))
    for i, (c, p, e) in enumerate(chains):
        a_r = jnp.where(incl, aall[i][CHUNK:], 0.0)
        if e == 0:
            bk_t = jnp.concatenate([blk(bh_s, c, p), blk(kh_s, c, p)], axis=0).T
        both = _mm(jnp.concatenate([a_r, bk_t[e * HEAD:(e + 1) * HEAD]], axis=0),
                   jnp.concatenate([z[i], vm[i]], axis=0))
        ry.append(jnp.where(key_mask[e], blk(rt_s, c, p), 0.0) + both[:CHUNK])
        gam_c = ge_s[c:c + 1, p * PAIR:(p + 1) * PAIR]
        mgk.append(both[CHUNK:] + jnp.where(eye_key[e], gam_c, 0.0))
    y_blocks = []
    for c in range(N_CHUNKS):
        ys = []
        for hd, (p, e) in enumerate(heads):
            i = c * N_HEADS + hd
            sp = state[hd]
            lhs = jnp.concatenate([ry[i], mgk[i]], axis=0)
            if e == 0:
                res = _mm(lhs[:, :HEAD], sp)
            else:
                res = _mm(lhs, jnp.concatenate([zeros_blk, sp], axis=0))
            ys.append(res[:CHUNK] + ry[i])
            state[hd] = res[CHUNK:] + jnp.where(key_mask[e], 0.0, mgk[i])
        y_blocks.append(jnp.concatenate(
            [jnp.where(key_mask[0], ys[2 * p + 1], ys[2 * p]) for p in range(N_HEADS // 2)],
            axis=1))
    y = jnp.concatenate(y_blocks, axis=0)

    mean = _mm(y, headsum) * (1.0 / HEAD)
    d = y - mean
    var = _mm(d * d, headsum) * (1.0 / HEAD)
    y = d * lax.rsqrt(var + GN_EPS) * gng_ref[0:1, :] + gnb_ref[0:1, :]
    y_b = (y + bonus) * gate_b

    yy = jnp.concatenate([y_a, y_b], axis=1).astype(BF16)
    xo = x + jnp.dot(yy, wout_ref[...], preferred_element_type=F32)
    inv_o = lax.rsqrt(jnp.mean(xo * xo, axis=-1, keepdims=True) + NORM_EPS)
    o_ref[0, tile_rows, :] = xo * inv_o * fg_ref[...]


def _block_kernel(x_ref, win_f, poolw_f, wup_f, aup_f, wout_f, headsum_ref, headswap_ref, cumtri_ref,
                  ng_ref, pscale_ref, mu_f, w0_ref, a0_ref, kk_ref, ka_ref, rk_ref, gng_f, gnb_f,
                  fg_ref,
                  o_ref,
                  shift_carry, pool_carry, state,
                  at_s, rt_s, bt_s, kt_s, bh_s, kh_s, v_s, ge_s,
                  win_ref, poolw_ref, mu_ref, lora_ref, gng_ref, gnb_ref, wout_ref):
    step = pl.program_id(1)

    @pl.when((pl.program_id(0) == 0) & (step == 0))
    def _():
        _prepare_weights(win_f, poolw_f, mu_f, wup_f, aup_f, gng_f, gnb_f, wout_f,
                         win_ref, poolw_ref, mu_ref, lora_ref, gng_ref, gnb_ref, wout_ref)

    def body(i, carry):
        _tile(step * STEP_TILES + i, pl.ds(pl.multiple_of(i * TILE, TILE), TILE),
              x_ref, ng_ref, pscale_ref, w0_ref, a0_ref, kk_ref, ka_ref, rk_ref, fg_ref,
              headsum_ref, headswap_ref, cumtri_ref, o_ref, shift_carry, pool_carry, state,
              at_s, rt_s, bt_s, kt_s, bh_s, kh_s, v_s, ge_s,
              win_ref, poolw_ref, mu_ref, lora_ref, gng_ref, gnb_ref, wout_ref)
        return carry

    lax.fori_loop(0, STEP_TILES, body, 0)


def _full(shape, single_buffer=False):
    index_map = lambda b, j: (0,) * len(shape)
    if single_buffer:
        return pl.BlockSpec(shape, index_map, pipeline_mode=pl.Buffered(1))
    return pl.BlockSpec(shape, index_map)


def kernel(x, norm_gain, w_in, pool_w, pool_scale, shift_mu, w0, w_up, a0, a_up, k_k, k_a, r_k,
           gn_gain, gn_bias, w_out, final_gain):
    batch, seq, d_model = x.shape
    step_rows = STEP_TILES * TILE
    assert norm_gain.shape[0] == 1 and d_model == D_MODEL and seq % step_rows == 0
    assert w_in.shape == (1, D_MODEL, D_IN)

    hid = jnp.arange(D_RWKV) // HEAD
    headsum = (hid[:, None] == hid[None, :]).astype(BF16)
    headsum_swap = (hid[:, None] == (hid[None, :] ^ 1)).astype(BF16)
    t = jnp.arange(TILE)
    same_chunk = (t[:, None] // CHUNK) == (t[None, :] // CHUNK)
    cumtri = (same_chunk & (t[:, None] >= t[None, :])).astype(BF16)

    operands = (
        x, w_in, pool_w, w_up, a_up, w_out, headsum, headsum_swap, cumtri,
        norm_gain, pool_scale, shift_mu.reshape(D_SHIFT // PAIR, PAIR), w0, a0, k_k, k_a,
        r_k.reshape(1, D_RWKV), gn_gain, gn_bias, final_gain.reshape(1, D_MODEL),
    )
    big = {1, 5}
    in_specs = [pl.BlockSpec((1, step_rows, D_MODEL), lambda b, j: (b, j, 0))]
    in_specs += [_full(op.shape, i in big) for i, op in enumerate(operands) if i > 0]
    seq_buf = lambda: pltpu.VMEM((TILE, D_RWKV), F32)
    return pl.pallas_call(
        _block_kernel,
        grid=(batch, seq // step_rows),
        in_specs=in_specs,
        out_specs=pl.BlockSpec((1, step_rows, D_MODEL), lambda b, j: (b, j, 0)),
        out_shape=jax.ShapeDtypeStruct(x.shape, x.dtype),
        scratch_shapes=[
            pltpu.VMEM((1, SHIFT_PAD), F32),
            pltpu.VMEM((POOL_HALO, D_POOL), F32),
            pltpu.VMEM((N_HEADS, HEAD, PAIR), F32),
        ] + [seq_buf() for _ in range(7)] + [
            pltpu.VMEM((8, D_RWKV), F32),
            pltpu.VMEM((D_MODEL, D_IN), BF16),
            pltpu.VMEM((len(POOL_WINDOWS), POOL_GROUP, POOL_GROUP), BF16),
            pltpu.VMEM((1, SHIFT_PAD), F32),
            pltpu.VMEM((2 * LORA, 2 * D_RWKV), BF16),
            pltpu.VMEM((8, D_RWKV), F32),
            pltpu.VMEM((8, D_RWKV), F32),
            pltpu.VMEM((D_MODEL, D_MODEL), BF16),
        ],
        compiler_params=pltpu.CompilerParams(
            dimension_semantics=("arbitrary", "arbitrary"),
            vmem_limit_bytes=VMEM_LIMIT_BYTES),
        name="hybrid_pool_rwkv7_block",
    )(*operands)
```

```python
import math

import jax
import jax.numpy as jnp
from jax import lax
from jax.experimental import pallas as pl
from jax.experimental.pallas import tpu as pltpu

F32 = jnp.float32
BF16 = jnp.bfloat16

D_MODEL = 1024
D_POOL = 512
POOL_WINDOWS = (2, 4, 8, 16)
POOL_GROUP = 128
POOL_HALO = 16
D_RWKV = 512
N_HEADS = 8
HEAD = 64
PAIR = 2 * HEAD
LORA = 64
D_SHIFT = 3 * D_RWKV + 2 * LORA
D_IN = 2 * D_POOL + D_SHIFT + D_RWKV
NORM_EPS = 1e-6
GN_EPS = 64e-5
CHUNK = 64
TILE = 256
STEP_TILES = 4
N_CHUNKS = TILE // CHUNK
DECAY_SCALE = math.exp(-0.5)
VMEM_LIMIT_BYTES = 56 * 1024 * 1024


def _mm(a, b):
    return jnp.dot(a.astype(BF16), b.astype(BF16), preferred_element_type=F32)


def _split(a, parts):
    out = []
    rem = a
    for _ in range(parts):
        p = rem.astype(BF16)
        out.append(p)
        rem = rem - p.astype(F32)
    return out


def _mm_exact_lhs(a_bf16, b, parts):
    acc = None
    for p in _split(b, parts):
        t = jnp.dot(a_bf16, p, preferred_element_type=F32)
        acc = t if acc is None else acc + t
    return acc


def _sigmoid(t):
    return 0.5 * jnp.tanh(0.5 * t) + 0.5


def _silu(t):
    return t * _sigmoid(t)


def _swap_pair_lanes(a):
    rows = a.shape[0]
    if rows < 8:
        a = jnp.broadcast_to(a[0:1], (8, a.shape[1]))
    out = jnp.concatenate([pltpu.roll(a[:, c:c + PAIR], HEAD, axis=1)
                           for c in range(0, a.shape[1], PAIR)], axis=1)
    return out[:rows]


def _prepare_weights(win_f, poolw_f, mu_f, wup_f, aup_f, gng_f, gnb_f, wout_f,
                     win_ref, poolw_ref, mu_ref, lora_ref, gng_ref, gnb_ref, wout_ref):
    v0 = 2 * D_POOL + 2 * D_RWKV
    g0 = 2 * D_POOL + D_SHIFT
    for c in range(0, D_IN, PAIR):
        w = win_f[0, :, c:c + PAIR]
        if v0 <= c < v0 + D_RWKV or c >= g0:
            w = pltpu.roll(w, HEAD, axis=1)
        win_ref[:, c:c + PAIR] = w.astype(BF16)
    wout_ref[0:D_POOL, :] = wout_f[0, 0:D_POOL, :].astype(BF16)
    for r in range(D_POOL, D_MODEL, PAIR):
        wout_ref[r:r + HEAD, :] = wout_f[0, r + HEAD:r + PAIR, :].astype(BF16)
        wout_ref[r + HEAD:r + PAIR, :] = wout_f[0, r:r + HEAD, :].astype(BF16)
    poolw_ref[...] = poolw_f[0].astype(BF16)
    lora_ref[...] = jnp.zeros_like(lora_ref)
    lora_ref[0:LORA, 0:D_RWKV] = wup_f[0].astype(BF16)
    lora_ref[LORA:2 * LORA, D_RWKV:2 * D_RWKV] = aup_f[0].astype(BF16)
    mu_ref[...] = mu_f[...]
    mu_ref[:, 2 * D_RWKV:3 * D_RWKV] = _swap_pair_lanes(mu_f[:, 2 * D_RWKV:3 * D_RWKV])
    gng_ref[...] = _swap_pair_lanes(gng_f[...])
    gnb_ref[...] = _swap_pair_lanes(gnb_f[...])


def _tile(j, tile_rows, x_ref, ng_ref, pscale_ref, w0_ref, a0_ref, kk_ref, ka_ref, rk_ref, fg_ref,
          headsum_ref, headswap_ref, cumtri_ref, o_ref, shift_carry, pool_carry, state,
          at_s, rt_s, bt_s, kt_s, bh_s, kh_s, v_s, ge_s,
          win_ref, poolw_ref, mu_ref, lora_ref, gng_ref, gnb_ref, wout_ref):
    @pl.when(j == 0)
    def _():
        shift_carry[...] = jnp.zeros_like(shift_carry)
        pool_carry[...] = jnp.zeros_like(pool_carry)
        state[...] = jnp.zeros_like(state)

    x = x_ref[0, tile_rows, :]
    inv = lax.rsqrt(jnp.mean(x * x, axis=-1, keepdims=True) + NORM_EPS)
    h = (x * inv * ng_ref[...]).astype(BF16)

    row = lax.broadcasted_iota(jnp.int32, (TILE, 1), 0)

    z_sh = jnp.dot(h, win_ref[:, 2 * D_POOL:2 * D_POOL + D_SHIFT], preferred_element_type=F32)
    prev = pltpu.roll(z_sh, 1, axis=0)
    head_rows = jnp.where(row[:8] == 0, shift_carry[...], prev[:8])
    prev = jnp.concatenate([head_rows, prev[8:]], axis=0)
    shift_carry[...] = z_sh[TILE - 1:TILE, :]
    sh = z_sh + mu_ref[...] * (prev - z_sh)

    z_pool = jnp.dot(h, win_ref[:, 0:2 * D_POOL], preferred_element_type=F32)
    u = z_pool[:, :D_POOL]
    g_a = z_pool[:, D_POOL:]
    ext = jnp.concatenate([pool_carry[...], u], axis=0)
    pool_carry[...] = u[TILE - POOL_HALO:, :]
    pos = (j * TILE + row).astype(F32)
    pooled = []
    s = ext
    for g, w in enumerate(POOL_WINDOWS):
        lo = g * POOL_GROUP
        s = s[:, (POOL_GROUP if g > 0 else 0):]
        s = s + pltpu.roll(s, w // 2, axis=0)
        inv_cnt = 1.0 / jnp.minimum(pos + 1.0, float(w))
        pooled.append(s[POOL_HALO:, :POOL_GROUP] * inv_cnt - u[:, lo:lo + POOL_GROUP])
    mixed = [_mm(pooled[g], poolw_ref[g]) for g in range(len(POOL_WINDOWS))]
    y_a = jnp.concatenate(mixed, axis=1) * pscale_ref[...] * _silu(g_a)

    r = sh[:, 0:D_RWKV]
    k = sh[:, D_RWKV:2 * D_RWKV]
    v = sh[:, 2 * D_RWKV:3 * D_RWKV]
    lr = sh[:, 3 * D_RWKV:]
    lane = lax.broadcasted_iota(jnp.int32, (1, 2 * LORA), 1)
    lr = jnp.where(lane < LORA, jnp.tanh(lr), lr)
    lora = _mm(lr, lora_ref[...])
    ww = w0_ref[...] + lora[:, :D_RWKV]
    a = _sigmoid(a0_ref[...] + lora[:, D_RWKV:])
    logw = -DECAY_SCALE * _sigmoid(ww)

    headsum = headsum_ref[...]
    kkv = k * kk_ref[...]
    ss = _mm(kkv * kkv, headsum)
    kkn = kkv * lax.rsqrt(jnp.maximum(ss, 1e-24))
    k2 = k * (1.0 + (a - 1.0) * ka_ref[...])
    bonus = _mm(r * k2 * rk_ref[...], headswap_ref[...]) * v

    gcum = _mm_exact_lhs(cumtri_ref[...], logw, 2)
    e_inv = jnp.exp(-gcum)
    kka = kkn * a
    at_s[...] = -kkn * jnp.exp(gcum - logw)
    rt_s[...] = r * jnp.exp(gcum)
    bt_s[...] = kka * e_inv
    kt_s[...] = k2 * e_inv
    v_s[...] = v
    for c in range(N_CHUNKS):
        rows = slice(c * CHUNK, (c + 1) * CHUNK)
        gam_c = jnp.exp(gcum[rows.stop - 1:rows.stop, :])
        ge_s[c:c + 1, :] = gam_c
        e_end = e_inv[rows] * gam_c
        bh_s[rows, :] = kka[rows] * e_end
        kh_s[rows, :] = k2[rows] * e_end
    g_b = jnp.dot(h, win_ref[:, 2 * D_POOL + D_SHIFT:], preferred_element_type=F32)
    gate_b = _silu(g_b)

    ri = lax.broadcasted_iota(jnp.int32, (CHUNK, PAIR), 0)
    ci = lax.broadcasted_iota(jnp.int32, (CHUNK, PAIR), 1)
    cm = ci % CHUNK
    left = ci < CHUNK
    strict_k = (ri > cm) & ~left
    strict_b = (ri > cm) & left
    incl = ri >= cm
    eye_r = jnp.where(ci == ri + CHUNK, 1.0, 0.0)
    lane_half = lax.broadcasted_iota(jnp.int32, (1, PAIR), 1) // HEAD
    key_mask = [lane_half == e for e in range(2)]
    eye_key = [ci == ri + e * HEAD for e in range(2)]
    zeros_blk = jnp.zeros((HEAD, PAIR), F32)
    keep_right = jnp.where(lane_half == 1, 1.0, 0.0)
    n_doubling = int(math.log2(CHUNK))
    heads = [(p, e) for p in range(N_HEADS // 2) for e in range(2)]
    chains = [(c, p, e) for c in range(N_CHUNKS) for p, e in heads]
    blk = lambda ref, c, p: ref[c * CHUNK:(c + 1) * CHUNK, p * PAIR:(p + 1) * PAIR]

    aall, vm, z, ry, tx, mgk = [], [], [], [], [], []
    for c, p, e in chains:
        if e == 0:
            lq = jnp.concatenate([blk(at_s, c, p), blk(rt_s, c, p)], axis=0)
            rk_t = jnp.concatenate([blk(bt_s, c, p), blk(kt_s, c, p)], axis=0).T
            rk_pair = jnp.concatenate(
                [jnp.concatenate([rk_t[:HEAD], zeros_blk], axis=1),
                 jnp.concatenate([zeros_blk, rk_t[HEAD:]], axis=1)], axis=0)
            scores = _mm(lq, rk_pair)
        aall.append(scores[:, e * PAIR:(e + 1) * PAIR])
        vm.append(jnp.where(key_mask[e], 0.0, blk(v_s, c, p)))
    for i in range(len(chains)):
        tx.append(jnp.where(strict_b, aall[i][:CHUNK], eye_r))
    per_level = -(-len(chains) // n_doubling)
    for level in range(n_doubling):
        for i in range(len(chains)):
            tx[i] = tx[i] * keep_right + _mm(tx[i][:, :CHUNK], tx[i])
        for i, (c, p, e) in list(enumerate(chains))[level * per_level:(level + 1) * per_level]:
            a_ak = jnp.where(strict_k, aall[i][:CHUNK], 0.0)
            av = _mm(a_ak, jnp.concatenate([vm[i], vm[i]], axis=0))
            z.append(jnp.where(key_mask[e], blk(at_s, c, p), av))
    for i in range(len(chains)):
        z[i] = _mm(tx[i], jnp.concatenate([z[i], z[i]], axis=0))
    for i, (c, p, e) in enumerate(chains):
        a_r = jnp.where(incl, aall[i][CHUNK:], 0.0)
        if e == 0:
            bk_t = jnp.concatenate([blk(bh_s, c, p), blk(kh_s, c, p)], axis=0).T
        both = _mm(jnp.concatenate([a_r, bk_t[e * HEAD:(e + 1) * HEAD]], axis=0),
                   jnp.concatenate([z[i], vm[i]], axis=0))
        ry.append(jnp.where(key_mask[e], blk(rt_s, c, p), 0.0) + both[:CHUNK])
        gam_c = ge_s[c:c + 1, p * PAIR:(p + 1) * PAIR]
        mgk.append(both[CHUNK:] + jnp.where(eye_key[e], gam_c, 0.0))
    y_blocks = []
    for c in range(N_CHUNKS):
        ys = []
        for hd, (p, e) in enumerate(heads):
            i = c * N_HEADS + hd
            sp = state[hd]
            lhs = jnp.concatenate([ry[i], mgk[i]], axis=0)
            if e == 0:
                res = _mm(lhs[:, :HEAD], sp)
            else:
                res = _mm(lhs, jnp.concatenate([zeros_blk, sp], axis=0))
            ys.append(res[:CHUNK] + ry[i])
            state[hd] = res[CHUNK:] + jnp.where(key_mask[e], 0.0, mgk[i])
        y_blocks.append(jnp.concatenate(
            [jnp.where(key_mask[0], ys[2 * p + 1], ys[2 * p]) for p in range(N_HEADS // 2)],
            axis=1))
    y = jnp.concatenate(y_blocks, axis=0)

    mean = _mm(y, headsum) * (1.0 / HEAD)
    d = y - mean
    var = _mm(d * d, headsum) * (1.0 / HEAD)
    y = d * lax.rsqrt(var + GN_EPS) * gng_ref[...] + gnb_ref[...]
    y_b = (y + bonus) * gate_b

    yy = jnp.concatenate([y_a, y_b], axis=1).astype(BF16)
    xo = x + jnp.dot(yy, wout_ref[...], preferred_element_type=F32)
    inv_o = lax.rsqrt(jnp.mean(xo * xo, axis=-1, keepdims=True) + NORM_EPS)
    o_ref[0, tile_rows, :] = xo * inv_o * fg_ref[...]


def _block_kernel(x_ref, ng_ref, win_f, poolw_f, pscale_ref, mu_f, w0_ref, wup_f,
                  a0_ref, aup_f, kk_ref, ka_ref, rk_ref, gng_f, gnb_f, wout_f, fg_ref,
                  headsum_ref, headswap_ref, cumtri_ref,
                  o_ref,
                  shift_carry, pool_carry, state,
                  at_s, rt_s, bt_s, kt_s, bh_s, kh_s, v_s, ge_s,
                  win_ref, poolw_ref, mu_ref, lora_ref, gng_ref, gnb_ref, wout_ref):
    step = pl.program_id(1)

    @pl.when((pl.program_id(0) == 0) & (step == 0))
    def _():
        _prepare_weights(win_f, poolw_f, mu_f, wup_f, aup_f, gng_f, gnb_f, wout_f,
                         win_ref, poolw_ref, mu_ref, lora_ref, gng_ref, gnb_ref, wout_ref)

    def body(i, carry):
        _tile(step * STEP_TILES + i, pl.ds(pl.multiple_of(i * TILE, TILE), TILE),
              x_ref, ng_ref, pscale_ref, w0_ref, a0_ref, kk_ref, ka_ref, rk_ref, fg_ref,
              headsum_ref, headswap_ref, cumtri_ref, o_ref, shift_carry, pool_carry, state,
              at_s, rt_s, bt_s, kt_s, bh_s, kh_s, v_s, ge_s,
              win_ref, poolw_ref, mu_ref, lora_ref, gng_ref, gnb_ref, wout_ref)
        return carry

    lax.fori_loop(0, STEP_TILES, body, 0)


def _full(shape, single_buffer=False):
    index_map = lambda b, j: (0,) * len(shape)
    if single_buffer:
        return pl.BlockSpec(shape, index_map, pipeline_mode=pl.Buffered(1))
    return pl.BlockSpec(shape, index_map)


def kernel(x, norm_gain, w_in, pool_w, pool_scale, shift_mu, w0, w_up, a0, a_up, k_k, k_a, r_k,
           gn_gain, gn_bias, w_out, final_gain):
    batch, seq, d_model = x.shape
    step_rows = STEP_TILES * TILE
    assert norm_gain.shape[0] == 1 and d_model == D_MODEL and seq % step_rows == 0
    assert w_in.shape == (1, D_MODEL, D_IN)

    hid = jnp.arange(D_RWKV) // HEAD
    headsum = (hid[:, None] == hid[None, :]).astype(BF16)
    headsum_swap = (hid[:, None] == (hid[None, :] ^ 1)).astype(BF16)
    t = jnp.arange(TILE)
    same_chunk = (t[:, None] // CHUNK) == (t[None, :] // CHUNK)
    cumtri = (same_chunk & (t[:, None] >= t[None, :])).astype(BF16)

    operands = (
        x, norm_gain, w_in, pool_w, pool_scale, shift_mu, w0, w_up, a0, a_up, k_k, k_a,
        r_k.reshape(1, D_RWKV), gn_gain, gn_bias, w_out, final_gain.reshape(1, D_MODEL),
        headsum, headsum_swap, cumtri,
    )
    big = {2, 15}
    in_specs = [pl.BlockSpec((1, step_rows, D_MODEL), lambda b, j: (b, j, 0))]
    in_specs += [_full(op.shape, i in big) for i, op in enumerate(operands) if i > 0]
    seq_buf = lambda: pltpu.VMEM((TILE, D_RWKV), F32)
    return pl.pallas_call(
        _block_kernel,
        grid=(batch, seq // step_rows),
        in_specs=in_specs,
        out_specs=pl.BlockSpec((1, step_rows, D_MODEL), lambda b, j: (b, j, 0)),
        out_shape=jax.ShapeDtypeStruct(x.shape, x.dtype),
        scratch_shapes=[
            pltpu.VMEM((1, D_SHIFT), F32),
            pltpu.VMEM((POOL_HALO, D_POOL), F32),
            pltpu.VMEM((N_HEADS, HEAD, PAIR), F32),
        ] + [seq_buf() for _ in range(7)] + [
            pltpu.VMEM((8, D_RWKV), F32),
            pltpu.VMEM((D_MODEL, D_IN), BF16),
            pltpu.VMEM((len(POOL_WINDOWS), POOL_GROUP, POOL_GROUP), BF16),
            pltpu.VMEM((1, D_SHIFT), F32),
            pltpu.VMEM((2 * LORA, 2 * D_RWKV), BF16),
            pltpu.VMEM((1, D_RWKV), F32),
            pltpu.VMEM((1, D_RWKV), F32),
            pltpu.VMEM((D_MODEL, D_MODEL), BF16),
        ],
        compiler_params=pltpu.CompilerParams(
            dimension_semantics=("arbitrary", "arbitrary"),
            vmem_limit_bytes=VMEM_LIMIT_BYTES),
        name="hybrid_pool_rwkv7_block",
    )(*operands)
```

```python
import math

import jax
import jax.numpy as jnp
from jax import lax
from jax.experimental import pallas as pl
from jax.experimental.pallas import tpu as pltpu

F32 = jnp.float32
BF16 = jnp.bfloat16

D_MODEL = 1024
D_POOL = 512
POOL_WINDOWS = (2, 4, 8, 16)
POOL_GROUP = 128
POOL_HALO = 16
D_RWKV = 512
N_HEADS = 8
HEAD = 64
PAIR = 2 * HEAD
LORA = 64
D_SHIFT = 3 * D_RWKV + 2 * LORA
D_IN = 2 * D_POOL + D_SHIFT + D_RWKV
NORM_EPS = 1e-6
GN_EPS = 64e-5
CHUNK = 64
TILE = 256
STEP_TILES = 4
N_CHUNKS = TILE // CHUNK
DECAY_SCALE = math.exp(-0.5)
VMEM_LIMIT_BYTES = 56 * 1024 * 1024


def _mm(a, b):
    return jnp.dot(a.astype(BF16), b.astype(BF16), preferred_element_type=F32)


def _split(a, parts):
    out = []
    rem = a
    for _ in range(parts):
        p = rem.astype(BF16)
        out.append(p)
        rem = rem - p.astype(F32)
    return out


def _mm_exact_lhs(a_bf16, b, parts):
    acc = None
    for p in _split(b, parts):
        t = jnp.dot(a_bf16, p, preferred_element_type=F32)
        acc = t if acc is None else acc + t
    return acc


def _sigmoid(t):
    return 0.5 * jnp.tanh(0.5 * t) + 0.5


def _silu(t):
    return t * _sigmoid(t)


def _swap_pair_lanes(a):
    rows = a.shape[0]
    if rows < 8:
        a = jnp.broadcast_to(a[0:1], (8, a.shape[1]))
    out = jnp.concatenate([pltpu.roll(a[:, c:c + PAIR], HEAD, axis=1)
                           for c in range(0, a.shape[1], PAIR)], axis=1)
    return out[:rows]


def _prepare_weights(win_f, poolw_f, mu_f, wup_f, aup_f, gng_f, gnb_f, wout_f,
                     win_ref, poolw_ref, mu_ref, lora_ref, gng_ref, gnb_ref, wout_ref):
    v0 = 2 * D_POOL + 2 * D_RWKV
    g0 = 2 * D_POOL + D_SHIFT
    for c in range(0, D_IN, PAIR):
        w = win_f[0, :, c:c + PAIR]
        if v0 <= c < v0 + D_RWKV or c >= g0:
            w = pltpu.roll(w, HEAD, axis=1)
        win_ref[:, c:c + PAIR] = w.astype(BF16)
    wout_ref[0:D_POOL, :] = wout_f[0, 0:D_POOL, :].astype(BF16)
    for r in range(D_POOL, D_MODEL, PAIR):
        wout_ref[r:r + HEAD, :] = wout_f[0, r + HEAD:r + PAIR, :].astype(BF16)
        wout_ref[r + HEAD:r + PAIR, :] = wout_f[0, r:r + HEAD, :].astype(BF16)
    poolw_ref[...] = poolw_f[0].astype(BF16)
    lora_ref[...] = jnp.zeros_like(lora_ref)
    lora_ref[0:LORA, 0:D_RWKV] = wup_f[0].astype(BF16)
    lora_ref[LORA:2 * LORA, D_RWKV:2 * D_RWKV] = aup_f[0].astype(BF16)
    mu_ref[...] = mu_f[...]
    mu_ref[:, 2 * D_RWKV:3 * D_RWKV] = _swap_pair_lanes(mu_f[:, 2 * D_RWKV:3 * D_RWKV])
    gng_ref[...] = _swap_pair_lanes(gng_f[...])
    gnb_ref[...] = _swap_pair_lanes(gnb_f[...])


def _tile(j, tile_rows, x_ref, ng_ref, pscale_ref, w0_ref, a0_ref, kk_ref, ka_ref, rk_ref, fg_ref,
          headsum_ref, headswap_ref, cumtri_ref, o_ref, shift_carry, pool_carry, state,
          at_s, rt_s, bt_s, kt_s, bh_s, kh_s, v_s, ge_s,
          win_ref, poolw_ref, mu_ref, lora_ref, gng_ref, gnb_ref, wout_ref):
    @pl.when(j == 0)
    def _():
        shift_carry[...] = jnp.zeros_like(shift_carry)
        pool_carry[...] = jnp.zeros_like(pool_carry)
        state[...] = jnp.zeros_like(state)

    x = x_ref[0, tile_rows, :]
    inv = lax.rsqrt(jnp.mean(x * x, axis=-1, keepdims=True) + NORM_EPS)
    h = (x * inv * ng_ref[...]).astype(BF16)

    row = lax.broadcasted_iota(jnp.int32, (TILE, 1), 0)

    z_sh = jnp.dot(h, win_ref[:, 2 * D_POOL:2 * D_POOL + D_SHIFT], preferred_element_type=F32)
    prev = pltpu.roll(z_sh, 1, axis=0)
    head_rows = jnp.where(row[:8] == 0, shift_carry[...], prev[:8])
    prev = jnp.concatenate([head_rows, prev[8:]], axis=0)
    shift_carry[...] = z_sh[TILE - 1:TILE, :]
    sh = z_sh + mu_ref[...] * (prev - z_sh)

    z_pool = jnp.dot(h, win_ref[:, 0:2 * D_POOL], preferred_element_type=F32)
    u = z_pool[:, :D_POOL]
    g_a = z_pool[:, D_POOL:]
    ext = jnp.concatenate([pool_carry[...], u], axis=0)
    pool_carry[...] = u[TILE - POOL_HALO:, :]
    pos = (j * TILE + row).astype(F32)
    pooled = []
    s = ext
    for g, w in enumerate(POOL_WINDOWS):
        lo = g * POOL_GROUP
        s = s[:, (POOL_GROUP if g > 0 else 0):]
        s = s + pltpu.roll(s, w // 2, axis=0)
        inv_cnt = 1.0 / jnp.minimum(pos + 1.0, float(w))
        pooled.append(s[POOL_HALO:, :POOL_GROUP] * inv_cnt - u[:, lo:lo + POOL_GROUP])
    mixed = [_mm(pooled[g], poolw_ref[g]) for g in range(len(POOL_WINDOWS))]
    y_a = jnp.concatenate(mixed, axis=1) * pscale_ref[...] * _silu(g_a)

    r = sh[:, 0:D_RWKV]
    k = sh[:, D_RWKV:2 * D_RWKV]
    v = sh[:, 2 * D_RWKV:3 * D_RWKV]
    lr = sh[:, 3 * D_RWKV:]
    lane = lax.broadcasted_iota(jnp.int32, (1, 2 * LORA), 1)
    lr = jnp.where(lane < LORA, jnp.tanh(lr), lr)
    lora = _mm(lr, lora_ref[...])
    ww = w0_ref[...] + lora[:, :D_RWKV]
    a = _sigmoid(a0_ref[...] + lora[:, D_RWKV:])
    logw = -DECAY_SCALE * _sigmoid(ww)

    headsum = headsum_ref[...]
    kkv = k * kk_ref[...]
    ss = _mm(kkv * kkv, headsum)
    kkn = kkv * lax.rsqrt(jnp.maximum(ss, 1e-24))
    k2 = k * (1.0 + (a - 1.0) * ka_ref[...])
    bonus = _mm(r * k2 * rk_ref[...], headswap_ref[...]) * v

    gcum = _mm_exact_lhs(cumtri_ref[...], logw, 2)
    e_inv = jnp.exp(-gcum)
    kka = kkn * a
    at_s[...] = -kkn * jnp.exp(gcum - logw)
    rt_s[...] = r * jnp.exp(gcum)
    bt_s[...] = kka * e_inv
    kt_s[...] = k2 * e_inv
    v_s[...] = v
    for c in range(N_CHUNKS):
        rows = slice(c * CHUNK, (c + 1) * CHUNK)
        gam_c = jnp.exp(gcum[rows.stop - 1:rows.stop, :])
        ge_s[c:c + 1, :] = gam_c
        e_end = e_inv[rows] * gam_c
        bh_s[rows, :] = kka[rows] * e_end
        kh_s[rows, :] = k2[rows] * e_end
    g_b = jnp.dot(h, win_ref[:, 2 * D_POOL + D_SHIFT:], preferred_element_type=F32)
    gate_b = _silu(g_b)

    ri = lax.broadcasted_iota(jnp.int32, (CHUNK, PAIR), 0)
    ci = lax.broadcasted_iota(jnp.int32, (CHUNK, PAIR), 1)
    cm = ci % CHUNK
    left = ci < CHUNK
    strict_k = (ri > cm) & ~left
    strict_b = (ri > cm) & left
    incl = ri >= cm
    eye_r = jnp.where(ci == ri + CHUNK, 1.0, 0.0)
    lane_half = lax.broadcasted_iota(jnp.int32, (1, PAIR), 1) // HEAD
    key_mask = [lane_half == e for e in range(2)]
    eye_key = [ci == ri + e * HEAD for e in range(2)]
    zeros_blk = jnp.zeros((HEAD, PAIR), F32)
    keep_right = jnp.where(lane_half == 1, 1.0, 0.0)
    n_doubling = int(math.log2(CHUNK))
    heads = [(p, e) for p in range(N_HEADS // 2) for e in range(2)]
    chains = [(c, p, e) for c in range(N_CHUNKS) for p, e in heads]
    blk = lambda ref, c, p: ref[c * CHUNK:(c + 1) * CHUNK, p * PAIR:(p + 1) * PAIR]

    aall, vm, z, ry, tx, mgk = [], [], [], [], [], []
    for c, p, e in chains:
        if e == 0:
            lq = jnp.concatenate([blk(at_s, c, p), blk(rt_s, c, p)], axis=0)
            rk_t = jnp.concatenate([blk(bt_s, c, p), blk(kt_s, c, p)], axis=0).T
            rk_pair = jnp.concatenate(
                [jnp.concatenate([rk_t[:HEAD], zeros_blk], axis=1),
                 jnp.concatenate([zeros_blk, rk_t[HEAD:]], axis=1)], axis=0)
            scores = _mm(lq, rk_pair)
        aall.append(scores[:, e * PAIR:(e + 1) * PAIR])
        vm.append(jnp.where(key_mask[e], 0.0, blk(v_s, c, p)))
    for i in range(len(chains)):
        tx.append(jnp.where(strict_b, aall[i][:CHUNK], eye_r))
    per_level = -(-len(chains) // n_doubling)
    for level in range(n_doubling):
        for i in range(len(chains)):
            tx[i] = tx[i] * keep_right + _mm(tx[i][:, :CHUNK], tx[i])
        for i, (c, p, e) in list(enumerate(chains))[level * per_level:(level + 1) * per_level]:
            a_ak = jnp.where(strict_k, aall[i][:CHUNK], 0.0)
            av = _mm(a_ak, jnp.concatenate([vm[i], vm[i]], axis=0))
            z.append(jnp.where(key_mask[e], blk(at_s, c, p), av))
    for i in range(len(chains)):
        z[i] = _mm(tx[i], jnp.concatenate([z[i], z[i]], axis=0))
    for i, (c, p, e) in enumerate(chains):
        a_r = jnp.where(incl, aall[i][CHUNK:], 0.0)
        if e == 0:
            bk_t = jnp.concatenate([blk(bh_s, c, p), blk(kh_s, c, p)], axis=0).T
        both = _mm(jnp.concatenate([a_r, bk_t[e * HEAD:(e + 1) * HEAD]], axis=0),
                   jnp.concatenate([z[i], vm[i]], axis=0))
        ry.append(jnp.where(key_mask[e], blk(rt_s, c, p), 0.0) + both[:CHUNK])
        gam_c = ge_s[c:c + 1, p * PAIR:(p + 1) * PAIR]
        mgk.append(both[CHUNK:] + jnp.where(eye_key[e], gam_c, 0.0))
    y_blocks = []
    for c in range(N_CHUNKS):
        ys = []
        for p in range(N_HEADS // 2):
            i0 = c * N_HEADS + 2 * p
            sp = state[p]
            rhs = jnp.concatenate([jnp.where(key_mask[0], 0.0, sp),
                                   jnp.where(key_mask[0], sp, 0.0)], axis=0)
            ry_pair = jnp.where(key_mask[0], ry[i0], ry[i0 + 1])
            mgk_pair = jnp.where(key_mask[0], mgk[i0], mgk[i0 + 1])
            res = _mm(jnp.concatenate([ry_pair, mgk_pair], axis=0), rhs)
            ys.append(res[:CHUNK] + jnp.where(key_mask[0], ry[i0 + 1], ry[i0]))
            state[p] = res[CHUNK:] + jnp.where(key_mask[0], mgk[i0 + 1], mgk[i0])
        y_blocks.append(jnp.concatenate(ys, axis=1))
    y = jnp.concatenate(y_blocks, axis=0)

    mean = _mm(y, headsum) * (1.0 / HEAD)
    d = y - mean
    var = _mm(d * d, headsum) * (1.0 / HEAD)
    y = d * lax.rsqrt(var + GN_EPS) * gng_ref[...] + gnb_ref[...]
    y_b = (y + bonus) * gate_b

    yy = jnp.concatenate([y_a, y_b], axis=1).astype(BF16)
    xo = x + jnp.dot(yy, wout_ref[...], preferred_element_type=F32)
    inv_o = lax.rsqrt(jnp.mean(xo * xo, axis=-1, keepdims=True) + NORM_EPS)
    o_ref[0, tile_rows, :] = xo * inv_o * fg_ref[...]


def _block_kernel(x_ref, ng_ref, win_f, poolw_f, pscale_ref, mu_f, w0_ref, wup_f,
                  a0_ref, aup_f, kk_ref, ka_ref, rk_ref, gng_f, gnb_f, wout_f, fg_ref,
                  headsum_ref, headswap_ref, cumtri_ref,
                  o_ref,
                  shift_carry, pool_carry, state,
                  at_s, rt_s, bt_s, kt_s, bh_s, kh_s, v_s, ge_s,
                  win_ref, poolw_ref, mu_ref, lora_ref, gng_ref, gnb_ref, wout_ref):
    step = pl.program_id(1)

    @pl.when((pl.program_id(0) == 0) & (step == 0))
    def _():
        _prepare_weights(win_f, poolw_f, mu_f, wup_f, aup_f, gng_f, gnb_f, wout_f,
                         win_ref, poolw_ref, mu_ref, lora_ref, gng_ref, gnb_ref, wout_ref)

    def body(i, carry):
        _tile(step * STEP_TILES + i, pl.ds(pl.multiple_of(i * TILE, TILE), TILE),
              x_ref, ng_ref, pscale_ref, w0_ref, a0_ref, kk_ref, ka_ref, rk_ref, fg_ref,
              headsum_ref, headswap_ref, cumtri_ref, o_ref, shift_carry, pool_carry, state,
              at_s, rt_s, bt_s, kt_s, bh_s, kh_s, v_s, ge_s,
              win_ref, poolw_ref, mu_ref, lora_ref, gng_ref, gnb_ref, wout_ref)
        return carry

    lax.fori_loop(0, STEP_TILES, body, 0)


def _full(shape, single_buffer=False):
    index_map = lambda b, j: (0,) * len(shape)
    if single_buffer:
        return pl.BlockSpec(shape, index_map, pipeline_mode=pl.Buffered(1))
    return pl.BlockSpec(shape, index_map)


def kernel(x, norm_gain, w_in, pool_w, pool_scale, shift_mu, w0, w_up, a0, a_up, k_k, k_a, r_k,
           gn_gain, gn_bias, w_out, final_gain):
    batch, seq, d_model = x.shape
    step_rows = STEP_TILES * TILE
    assert norm_gain.shape[0] == 1 and d_model == D_MODEL and seq % step_rows == 0
    assert w_in.shape == (1, D_MODEL, D_IN)

    hid = jnp.arange(D_RWKV) // HEAD
    headsum = (hid[:, None] == hid[None, :]).astype(BF16)
    headsum_swap = (hid[:, None] == (hid[None, :] ^ 1)).astype(BF16)
    t = jnp.arange(TILE)
    same_chunk = (t[:, None] // CHUNK) == (t[None, :] // CHUNK)
    cumtri = (same_chunk & (t[:, None] >= t[None, :])).astype(BF16)

    operands = (
        x, norm_gain, w_in, pool_w, pool_scale, shift_mu, w0, w_up, a0, a_up, k_k, k_a,
        r_k.reshape(1, D_RWKV), gn_gain, gn_bias, w_out, final_gain.reshape(1, D_MODEL),
        headsum, headsum_swap, cumtri,
    )
    big = {2, 15}
    in_specs = [pl.BlockSpec((1, step_rows, D_MODEL), lambda b, j: (b, j, 0))]
    in_specs += [_full(op.shape, i in big) for i, op in enumerate(operands) if i > 0]
    seq_buf = lambda: pltpu.VMEM((TILE, D_RWKV), F32)
    return pl.pallas_call(
        _block_kernel,
        grid=(batch, seq // step_rows),
        in_specs=in_specs,
        out_specs=pl.BlockSpec((1, step_rows, D_MODEL), lambda b, j: (b, j, 0)),
        out_shape=jax.ShapeDtypeStruct(x.shape, x.dtype),
        scratch_shapes=[
            pltpu.VMEM((1, D_SHIFT), F32),
            pltpu.VMEM((POOL_HALO, D_POOL), F32),
            pltpu.VMEM((N_HEADS // 2, HEAD, PAIR), F32),
        ] + [seq_buf() for _ in range(7)] + [
            pltpu.VMEM((8, D_RWKV), F32),
            pltpu.VMEM((D_MODEL, D_IN), BF16),
            pltpu.VMEM((len(POOL_WINDOWS), POOL_GROUP, POOL_GROUP), BF16),
            pltpu.VMEM((1, D_SHIFT), F32),
            pltpu.VMEM((2 * LORA, 2 * D_RWKV), BF16),
            pltpu.VMEM((1, D_RWKV), F32),
            pltpu.VMEM((1, D_RWKV), F32),
            pltpu.VMEM((D_MODEL, D_MODEL), BF16),
        ],
        compiler_params=pltpu.CompilerParams(
            dimension_semantics=("arbitrary", "arbitrary"),
            vmem_limit_bytes=VMEM_LIMIT_BYTES),
        name="hybrid_pool_rwkv7_block",
    )(*operands)
```

```python
import math

import jax
import jax.numpy as jnp
from jax import lax
from jax.experimental import pallas as pl
from jax.experimental.pallas import tpu as pltpu

F32 = jnp.float32
BF16 = jnp.bfloat16

D_MODEL = 1024
D_POOL = 512
POOL_WINDOWS = (2, 4, 8, 16)
POOL_GROUP = 128
POOL_HALO = 16
D_RWKV = 512
N_HEADS = 8
HEAD = 64
PAIR = 2 * HEAD
LORA = 64
D_SHIFT = 3 * D_RWKV + 2 * LORA
D_IN = 2 * D_POOL + D_SHIFT + D_RWKV
NORM_EPS = 1e-6
GN_EPS = 64e-5
CHUNK = 64
TILE = 256
STEP_TILES = 4
N_CHUNKS = TILE // CHUNK
DECAY_SCALE = math.exp(-0.5)
VMEM_LIMIT_BYTES = 56 * 1024 * 1024


def _mm(a, b):
    return jnp.dot(a.astype(BF16), b.astype(BF16), preferred_element_type=F32)


def _split(a, parts):
    out = []
    rem = a
    for _ in range(parts):
        p = rem.astype(BF16)
        out.append(p)
        rem = rem - p.astype(F32)
    return out


def _mm_exact_lhs(a_bf16, b, parts):
    acc = None
    for p in _split(b, parts):
        t = jnp.dot(a_bf16, p, preferred_element_type=F32)
        acc = t if acc is None else acc + t
    return acc


def _pair_sum(a, ones_blk):
    return jnp.concatenate([_mm(a[:, c:c + PAIR], ones_blk) for c in range(0, a.shape[1], PAIR)],
                           axis=1)


def _sigmoid(t):
    return 0.5 * jnp.tanh(0.5 * t) + 0.5


def _silu(t):
    return t * _sigmoid(t)


def _swap_pair_lanes(a):
    rows = a.shape[0]
    if rows < 8:
        a = jnp.broadcast_to(a[0:1], (8, a.shape[1]))
    out = jnp.concatenate([pltpu.roll(a[:, c:c + PAIR], HEAD, axis=1)
                           for c in range(0, a.shape[1], PAIR)], axis=1)
    return out[:rows]


def _prepare_weights(win_f, poolw_f, mu_f, wup_f, aup_f, gng_f, gnb_f, wout_f,
                     win_ref, poolw_ref, mu_ref, lora_ref, gng_ref, gnb_ref, wout_ref):
    v0 = 2 * D_POOL + 2 * D_RWKV
    g0 = 2 * D_POOL + D_SHIFT
    for c in range(0, D_IN, PAIR):
        w = win_f[0, :, c:c + PAIR]
        if v0 <= c < v0 + D_RWKV or c >= g0:
            w = pltpu.roll(w, HEAD, axis=1)
        win_ref[:, c:c + PAIR] = w.astype(BF16)
    wout_ref[0:D_POOL, :] = wout_f[0, 0:D_POOL, :].astype(BF16)
    for r in range(D_POOL, D_MODEL, PAIR):
        wout_ref[r:r + HEAD, :] = wout_f[0, r + HEAD:r + PAIR, :].astype(BF16)
        wout_ref[r + HEAD:r + PAIR, :] = wout_f[0, r:r + HEAD, :].astype(BF16)
    poolw_ref[...] = poolw_f[0].astype(BF16)
    lora_ref[...] = jnp.zeros_like(lora_ref)
    lora_ref[0:LORA, 0:D_RWKV] = wup_f[0].astype(BF16)
    lora_ref[LORA:2 * LORA, D_RWKV:2 * D_RWKV] = aup_f[0].astype(BF16)
    mu_ref[...] = mu_f[...]
    mu_ref[:, 2 * D_RWKV:3 * D_RWKV] = _swap_pair_lanes(mu_f[:, 2 * D_RWKV:3 * D_RWKV])
    gng_ref[...] = _swap_pair_lanes(gng_f[...])
    gnb_ref[...] = _swap_pair_lanes(gnb_f[...])


def _tile(j, tile_rows, x_ref, ng_ref, pscale_ref, w0_ref, a0_ref, kk_ref, ka_ref, rk_ref, fg_ref,
          headsum_ref, headswap_ref, cumtri_ref, o_ref, shift_carry, pool_carry, state,
          at_s, rt_s, bt_s, kt_s, bh_s, kh_s, v_s, ge_s,
          win_ref, poolw_ref, mu_ref, lora_ref, gng_ref, gnb_ref, wout_ref):
    @pl.when(j == 0)
    def _():
        shift_carry[...] = jnp.zeros_like(shift_carry)
        pool_carry[...] = jnp.zeros_like(pool_carry)
        state[...] = jnp.zeros_like(state)

    x = x_ref[0, tile_rows, :]
    inv = lax.rsqrt(jnp.mean(x * x, axis=-1, keepdims=True) + NORM_EPS)
    h = (x * inv * ng_ref[...]).astype(BF16)

    row = lax.broadcasted_iota(jnp.int32, (TILE, 1), 0)

    z_sh = jnp.dot(h, win_ref[:, 2 * D_POOL:2 * D_POOL + D_SHIFT], preferred_element_type=F32)
    prev = pltpu.roll(z_sh, 1, axis=0)
    head_rows = jnp.where(row[:8] == 0, shift_carry[...], prev[:8])
    prev = jnp.concatenate([head_rows, prev[8:]], axis=0)
    shift_carry[...] = z_sh[TILE - 1:TILE, :]
    sh = z_sh + mu_ref[...] * (prev - z_sh)

    z_pool = jnp.dot(h, win_ref[:, 0:2 * D_POOL], preferred_element_type=F32)
    u = z_pool[:, :D_POOL]
    g_a = z_pool[:, D_POOL:]
    ext = jnp.concatenate([pool_carry[...], u], axis=0)
    pool_carry[...] = u[TILE - POOL_HALO:, :]
    pos = (j * TILE + row).astype(F32)
    pooled = []
    s = ext
    for g, w in enumerate(POOL_WINDOWS):
        lo = g * POOL_GROUP
        s = s[:, (POOL_GROUP if g > 0 else 0):]
        s = s + pltpu.roll(s, w // 2, axis=0)
        inv_cnt = 1.0 / jnp.minimum(pos + 1.0, float(w))
        pooled.append(s[POOL_HALO:, :POOL_GROUP] * inv_cnt - u[:, lo:lo + POOL_GROUP])
    mixed = [_mm(pooled[g], poolw_ref[g]) for g in range(len(POOL_WINDOWS))]
    y_a = jnp.concatenate(mixed, axis=1) * pscale_ref[...] * _silu(g_a)

    r = sh[:, 0:D_RWKV]
    k = sh[:, D_RWKV:2 * D_RWKV]
    v = sh[:, 2 * D_RWKV:3 * D_RWKV]
    lr = sh[:, 3 * D_RWKV:]
    lane = lax.broadcasted_iota(jnp.int32, (1, 2 * LORA), 1)
    lr = jnp.where(lane < LORA, jnp.tanh(lr), lr)
    lora = _mm(lr, lora_ref[...])
    ww = w0_ref[...] + lora[:, :D_RWKV]
    a = _sigmoid(a0_ref[...] + lora[:, D_RWKV:])
    logw = -DECAY_SCALE * _sigmoid(ww)

    headsum = headsum_ref[...]
    kkv = k * kk_ref[...]
    ss = _pair_sum(kkv * kkv, headsum)
    kkn = kkv * lax.rsqrt(jnp.maximum(ss, 1e-24))
    k2 = k * (1.0 + (a - 1.0) * ka_ref[...])
    bonus = _pair_sum(r * k2 * rk_ref[...], headswap_ref[...]) * v

    gcum = _mm_exact_lhs(cumtri_ref[...], logw, 2)
    e_inv = jnp.exp(-gcum)
    kka = kkn * a
    at_s[...] = -kkn * jnp.exp(gcum - logw)
    rt_s[...] = r * jnp.exp(gcum)
    bt_s[...] = kka * e_inv
    kt_s[...] = k2 * e_inv
    v_s[...] = v
    for c in range(N_CHUNKS):
        rows = slice(c * CHUNK, (c + 1) * CHUNK)
        gam_c = jnp.exp(gcum[rows.stop - 1:rows.stop, :])
        ge_s[c:c + 1, :] = gam_c
        e_end = e_inv[rows] * gam_c
        bh_s[rows, :] = kka[rows] * e_end
        kh_s[rows, :] = k2[rows] * e_end
    g_b = jnp.dot(h, win_ref[:, 2 * D_POOL + D_SHIFT:], preferred_element_type=F32)
    gate_b = _silu(g_b)

    ri = lax.broadcasted_iota(jnp.int32, (CHUNK, PAIR), 0)
    ci = lax.broadcasted_iota(jnp.int32, (CHUNK, PAIR), 1)
    cm = ci % CHUNK
    left = ci < CHUNK
    strict_k = (ri > cm) & ~left
    strict_b = (ri > cm) & left
    incl = ri >= cm
    eye_r = jnp.where(ci == ri + CHUNK, 1.0, 0.0)
    lane_half = lax.broadcasted_iota(jnp.int32, (1, PAIR), 1) // HEAD
    key_mask = [lane_half == e for e in range(2)]
    eye_key = [ci == ri + e * HEAD for e in range(2)]
    zeros_blk = jnp.zeros((HEAD, PAIR), F32)
    keep_right = jnp.where(lane_half == 1, 1.0, 0.0)
    n_doubling = int(math.log2(CHUNK))
    heads = [(p, e) for p in range(N_HEADS // 2) for e in range(2)]
    chains = [(c, p, e) for c in range(N_CHUNKS) for p, e in heads]
    blk = lambda ref, c, p: ref[c * CHUNK:(c + 1) * CHUNK, p * PAIR:(p + 1) * PAIR]

    aall, vm, z, ry, tx, mgk = [], [], [], [], [], []
    for c, p, e in chains:
        if e == 0:
            lq = jnp.concatenate([blk(at_s, c, p), blk(rt_s, c, p)], axis=0)
            rk_t = jnp.concatenate([blk(bt_s, c, p), blk(kt_s, c, p)], axis=0).T
            rk_pair = jnp.concatenate(
                [jnp.concatenate([rk_t[:HEAD], zeros_blk], axis=1),
                 jnp.concatenate([zeros_blk, rk_t[HEAD:]], axis=1)], axis=0)
            scores = _mm(lq, rk_pair)
        aall.append(scores[:, e * PAIR:(e + 1) * PAIR])
        vm.append(jnp.where(key_mask[e], 0.0, blk(v_s, c, p)))
    for i in range(len(chains)):
        tx.append(jnp.where(strict_b, aall[i][:CHUNK], eye_r))
    per_level = -(-len(chains) // n_doubling)
    for level in range(n_doubling):
        for i in range(len(chains)):
            tx[i] = tx[i] * keep_right + _mm(tx[i][:, :CHUNK], tx[i])
        for i, (c, p, e) in list(enumerate(chains))[level * per_level:(level + 1) * per_level]:
            a_ak = jnp.where(strict_k, aall[i][:CHUNK], 0.0)
            av = _mm(a_ak, jnp.concatenate([vm[i], vm[i]], axis=0))
            z.append(jnp.where(key_mask[e], blk(at_s, c, p), av))
    for i in range(len(chains)):
        z[i] = _mm(tx[i], jnp.concatenate([z[i], z[i]], axis=0))
    for i, (c, p, e) in enumerate(chains):
        a_r = jnp.where(incl, aall[i][CHUNK:], 0.0)
        if e == 0:
            bk_t = jnp.concatenate([blk(bh_s, c, p), blk(kh_s, c, p)], axis=0).T
        both = _mm(jnp.concatenate([a_r, bk_t[e * HEAD:(e + 1) * HEAD]], axis=0),
                   jnp.concatenate([z[i], vm[i]], axis=0))
        ry.append(jnp.where(key_mask[e], blk(rt_s, c, p), 0.0) + both[:CHUNK])
        gam_c = ge_s[c:c + 1, p * PAIR:(p + 1) * PAIR]
        mgk.append(both[CHUNK:] + jnp.where(eye_key[e], gam_c, 0.0))
    y_blocks = []
    for c in range(N_CHUNKS):
        ys = []
        for p in range(N_HEADS // 2):
            i0 = c * N_HEADS + 2 * p
            sp = state[p]
            rhs = jnp.concatenate([jnp.where(key_mask[0], 0.0, sp),
                                   jnp.where(key_mask[0], sp, 0.0)], axis=0)
            ry_pair = jnp.where(key_mask[0], ry[i0], ry[i0 + 1])
            mgk_pair = jnp.where(key_mask[0], mgk[i0], mgk[i0 + 1])
            res = _mm(jnp.concatenate([ry_pair, mgk_pair], axis=0), rhs)
            ys.append(res[:CHUNK] + jnp.where(key_mask[0], ry[i0 + 1], ry[i0]))
            state[p] = res[CHUNK:] + jnp.where(key_mask[0], mgk[i0 + 1], mgk[i0])
        y_blocks.append(jnp.concatenate(ys, axis=1))
    y = jnp.concatenate(y_blocks, axis=0)

    mean = _pair_sum(y, headsum) * (1.0 / HEAD)
    d = y - mean
    var = _pair_sum(d * d, headsum) * (1.0 / HEAD)
    y = d * lax.rsqrt(var + GN_EPS) * gng_ref[...] + gnb_ref[...]
    y_b = (y + bonus) * gate_b

    yy = jnp.concatenate([y_a, y_b], axis=1).astype(BF16)
    xo = x + jnp.dot(yy, wout_ref[...], preferred_element_type=F32)
    inv_o = lax.rsqrt(jnp.mean(xo * xo, axis=-1, keepdims=True) + NORM_EPS)
    o_ref[0, tile_rows, :] = xo * inv_o * fg_ref[...]


def _block_kernel(x_ref, ng_ref, win_f, poolw_f, pscale_ref, mu_f, w0_ref, wup_f,
                  a0_ref, aup_f, kk_ref, ka_ref, rk_ref, gng_f, gnb_f, wout_f, fg_ref,
                  headsum_ref, headswap_ref, cumtri_ref,
                  o_ref,
                  shift_carry, pool_carry, state,
                  at_s, rt_s, bt_s, kt_s, bh_s, kh_s, v_s, ge_s,
                  win_ref, poolw_ref, mu_ref, lora_ref, gng_ref, gnb_ref, wout_ref):
    step = pl.program_id(1)

    @pl.when((pl.program_id(0) == 0) & (step == 0))
    def _():
        _prepare_weights(win_f, poolw_f, mu_f, wup_f, aup_f, gng_f, gnb_f, wout_f,
                         win_ref, poolw_ref, mu_ref, lora_ref, gng_ref, gnb_ref, wout_ref)

    def body(i, carry):
        _tile(step * STEP_TILES + i, pl.ds(pl.multiple_of(i * TILE, TILE), TILE),
              x_ref, ng_ref, pscale_ref, w0_ref, a0_ref, kk_ref, ka_ref, rk_ref, fg_ref,
              headsum_ref, headswap_ref, cumtri_ref, o_ref, shift_carry, pool_carry, state,
              at_s, rt_s, bt_s, kt_s, bh_s, kh_s, v_s, ge_s,
              win_ref, poolw_ref, mu_ref, lora_ref, gng_ref, gnb_ref, wout_ref)
        return carry

    lax.fori_loop(0, STEP_TILES, body, 0)


def _full(shape, single_buffer=False):
    index_map = lambda b, j: (0,) * len(shape)
    if single_buffer:
        return pl.BlockSpec(shape, index_map, pipeline_mode=pl.Buffered(1))
    return pl.BlockSpec(shape, index_map)


def kernel(x, norm_gain, w_in, pool_w, pool_scale, shift_mu, w0, w_up, a0, a_up, k_k, k_a, r_k,
           gn_gain, gn_bias, w_out, final_gain):
    batch, seq, d_model = x.shape
    step_rows = STEP_TILES * TILE
    assert norm_gain.shape[0] == 1 and d_model == D_MODEL and seq % step_rows == 0
    assert w_in.shape == (1, D_MODEL, D_IN)

    hid = jnp.arange(PAIR) // HEAD
    headsum = (hid[:, None] == hid[None, :]).astype(BF16)
    headsum_swap = (hid[:, None] != hid[None, :]).astype(BF16)
    t = jnp.arange(TILE)
    same_chunk = (t[:, None] // CHUNK) == (t[None, :] // CHUNK)
    cumtri = (same_chunk & (t[:, None] >= t[None, :])).astype(BF16)

    operands = (
        x, norm_gain, w_in, pool_w, pool_scale, shift_mu, w0, w_up, a0, a_up, k_k, k_a,
        r_k.reshape(1, D_RWKV), gn_gain, gn_bias, w_out, final_gain.reshape(1, D_MODEL),
        headsum, headsum_swap, cumtri,
    )
    big = {2, 15}
    in_specs = [pl.BlockSpec((1, step_rows, D_MODEL), lambda b, j: (b, j, 0))]
    in_specs += [_full(op.shape, i in big) for i, op in enumerate(operands) if i > 0]
    seq_buf = lambda: pltpu.VMEM((TILE, D_RWKV), F32)
    return pl.pallas_call(
        _block_kernel,
        grid=(batch, seq // step_rows),
        in_specs=in_specs,
        out_specs=pl.BlockSpec((1, step_rows, D_MODEL), lambda b, j: (b, j, 0)),
        out_shape=jax.ShapeDtypeStruct(x.shape, x.dtype),
        scratch_shapes=[
            pltpu.VMEM((1, D_SHIFT), F32),
            pltpu.VMEM((POOL_HALO, D_POOL), F32),
            pltpu.VMEM((N_HEADS // 2, HEAD, PAIR), F32),
        ] + [seq_buf() for _ in range(7)] + [
            pltpu.VMEM((8, D_RWKV), F32),
            pltpu.VMEM((D_MODEL, D_IN), BF16),
            pltpu.VMEM((len(POOL_WINDOWS), POOL_GROUP, POOL_GROUP), BF16),
            pltpu.VMEM((1, D_SHIFT), F32),
            pltpu.VMEM((2 * LORA, 2 * D_RWKV), BF16),
            pltpu.VMEM((1, D_RWKV), F32),
            pltpu.VMEM((1, D_RWKV), F32),
            pltpu.VMEM((D_MODEL, D_MODEL), BF16),
        ],
        compiler_params=pltpu.CompilerParams(
            dimension_semantics=("arbitrary", "arbitrary"),
            vmem_limit_bytes=VMEM_LIMIT_BYTES),
        name="hybrid_pool_rwkv7_block",
    )(*operands)
```

```python
import math

import jax
import jax.numpy as jnp
from jax import lax
from jax.experimental import pallas as pl
from jax.experimental.pallas import tpu as pltpu

F32 = jnp.float32
BF16 = jnp.bfloat16

D_MODEL = 1024
D_POOL = 512
POOL_WINDOWS = (2, 4, 8, 16)
POOL_GROUP = 128
POOL_HALO = 16
D_RWKV = 512
N_HEADS = 8
HEAD = 64
PAIR = 2 * HEAD
LORA = 64
D_SHIFT = 3 * D_RWKV + 2 * LORA
D_IN = 2 * D_POOL + D_SHIFT + D_RWKV
NORM_EPS = 1e-6
GN_EPS = 64e-5
CHUNK = 64
TILE = 256
STEP_TILES = 4
N_CHUNKS = TILE // CHUNK
DECAY_SCALE = math.exp(-0.5)
MXU_ROWS = 32
VMEM_LIMIT_BYTES = 56 * 1024 * 1024


def _mm(a, b):
    return jnp.dot(a.astype(BF16), b.astype(BF16), preferred_element_type=F32)


def _split(a, parts):
    out = []
    rem = a
    for _ in range(parts):
        p = rem.astype(BF16)
        out.append(p)
        rem = rem - p.astype(F32)
    return out


def _mm_exact_lhs(a_bf16, b, parts):
    acc = None
    for p in _split(b, parts):
        t = jnp.dot(a_bf16, p, preferred_element_type=F32)
        acc = t if acc is None else acc + t
    return acc


def _pair_sum(a, ones_blk):
    return jnp.concatenate([_mm(a[:, c:c + PAIR], ones_blk) for c in range(0, a.shape[1], PAIR)],
                           axis=1)


def _sigmoid(t):
    return 0.5 * jnp.tanh(0.5 * t) + 0.5


def _silu(t):
    return t * _sigmoid(t)


def _swap_pair_lanes(a):
    rows = a.shape[0]
    if rows < 8:
        a = jnp.broadcast_to(a[0:1], (8, a.shape[1]))
    out = jnp.concatenate([pltpu.roll(a[:, c:c + PAIR], HEAD, axis=1)
                           for c in range(0, a.shape[1], PAIR)], axis=1)
    return out[:rows]


def _prepare_weights(win_f, poolw_f, mu_f, wup_f, aup_f, gng_f, gnb_f, wout_f,
                     win_ref, poolw_ref, mu_ref, lora_ref, gng_ref, gnb_ref, wout_ref):
    v0 = 2 * D_POOL + 2 * D_RWKV
    g0 = 2 * D_POOL + D_SHIFT
    for c in range(0, D_IN, PAIR):
        w = win_f[0, :, c:c + PAIR]
        if v0 <= c < v0 + D_RWKV or c >= g0:
            w = pltpu.roll(w, HEAD, axis=1)
        win_ref[:, c:c + PAIR] = w.astype(BF16)
    wout_ref[0:D_POOL, :] = wout_f[0, 0:D_POOL, :].astype(BF16)
    for r in range(D_POOL, D_MODEL, PAIR):
        wout_ref[r:r + HEAD, :] = wout_f[0, r + HEAD:r + PAIR, :].astype(BF16)
        wout_ref[r + HEAD:r + PAIR, :] = wout_f[0, r:r + HEAD, :].astype(BF16)
    poolw_ref[...] = poolw_f[0].astype(BF16)
    lora_ref[...] = jnp.zeros_like(lora_ref)
    lora_ref[0:LORA, 0:D_RWKV] = wup_f[0].astype(BF16)
    lora_ref[LORA:2 * LORA, D_RWKV:2 * D_RWKV] = aup_f[0].astype(BF16)
    mu_ref[...] = mu_f[...]
    mu_ref[:, 2 * D_RWKV:3 * D_RWKV] = _swap_pair_lanes(mu_f[:, 2 * D_RWKV:3 * D_RWKV])
    gng_ref[...] = _swap_pair_lanes(gng_f[...])
    gnb_ref[...] = _swap_pair_lanes(gnb_f[...])


def _tile(j, tile_rows, x_ref, ng_ref, pscale_ref, w0_ref, a0_ref, kk_ref, ka_ref, rk_ref, fg_ref,
          headsum_ref, headswap_ref, cumtri_ref, o_ref, shift_carry, pool_carry, state,
          at_s, rt_s, bt_s, kt_s, bh_s, kh_s, v_s, ge_s,
          win_ref, poolw_ref, mu_ref, lora_ref, gng_ref, gnb_ref, wout_ref):
    @pl.when(j == 0)
    def _():
        shift_carry[...] = jnp.zeros_like(shift_carry)
        pool_carry[...] = jnp.zeros_like(pool_carry)
        state[...] = jnp.zeros_like(state)

    x = x_ref[0, tile_rows, :]
    inv = lax.rsqrt(jnp.mean(x * x, axis=-1, keepdims=True) + NORM_EPS)
    h = (x * inv * ng_ref[...]).astype(BF16)

    row = lax.broadcasted_iota(jnp.int32, (TILE, 1), 0)

    z_sh = jnp.dot(h, win_ref[:, 2 * D_POOL:2 * D_POOL + D_SHIFT], preferred_element_type=F32)
    prev = pltpu.roll(z_sh, 1, axis=0)
    head_rows = jnp.where(row[:8] == 0, shift_carry[...], prev[:8])
    prev = jnp.concatenate([head_rows, prev[8:]], axis=0)
    shift_carry[...] = z_sh[TILE - 1:TILE, :]
    sh = z_sh + mu_ref[...] * (prev - z_sh)

    z_pool = jnp.dot(h, win_ref[:, 0:2 * D_POOL], preferred_element_type=F32)
    u = z_pool[:, :D_POOL]
    g_a = z_pool[:, D_POOL:]
    ext = jnp.concatenate([pool_carry[...], u], axis=0)
    pool_carry[...] = u[TILE - POOL_HALO:, :]
    pos = (j * TILE + row).astype(F32)
    pooled = []
    s = ext
    for g, w in enumerate(POOL_WINDOWS):
        lo = g * POOL_GROUP
        s = s[:, (POOL_GROUP if g > 0 else 0):]
        s = s + pltpu.roll(s, w // 2, axis=0)
        inv_cnt = 1.0 / jnp.minimum(pos + 1.0, float(w))
        pooled.append(s[POOL_HALO:, :POOL_GROUP] * inv_cnt - u[:, lo:lo + POOL_GROUP])
    mixed = [_mm(pooled[g], poolw_ref[g]) for g in range(len(POOL_WINDOWS))]
    y_a = jnp.concatenate(mixed, axis=1) * pscale_ref[...] * _silu(g_a)

    r = sh[:, 0:D_RWKV]
    k = sh[:, D_RWKV:2 * D_RWKV]
    v = sh[:, 2 * D_RWKV:3 * D_RWKV]
    lr = sh[:, 3 * D_RWKV:]
    lane = lax.broadcasted_iota(jnp.int32, (1, 2 * LORA), 1)
    lr = jnp.where(lane < LORA, jnp.tanh(lr), lr)
    lora = _mm(lr, lora_ref[...])
    ww = w0_ref[...] + lora[:, :D_RWKV]
    a = _sigmoid(a0_ref[...] + lora[:, D_RWKV:])
    logw = -DECAY_SCALE * _sigmoid(ww)

    headsum = headsum_ref[...]
    kkv = k * kk_ref[...]
    ss = _pair_sum(kkv * kkv, headsum)
    kkn = kkv * lax.rsqrt(jnp.maximum(ss, 1e-24))
    k2 = k * (1.0 + (a - 1.0) * ka_ref[...])
    bonus = _pair_sum(r * k2 * rk_ref[...], headswap_ref[...]) * v

    gcum = _mm_exact_lhs(cumtri_ref[...], logw, 2)
    e_inv = jnp.exp(-gcum)
    kka = kkn * a
    at_s[...] = -kkn * jnp.exp(gcum - logw)
    rt_s[...] = r * jnp.exp(gcum)
    bt_s[...] = kka * e_inv
    kt_s[...] = k2 * e_inv
    v_s[...] = v
    for c in range(N_CHUNKS):
        rows = slice(c * CHUNK, (c + 1) * CHUNK)
        gam_c = jnp.exp(gcum[rows.stop - 1:rows.stop, :])
        ge_s[c:c + 1, :] = gam_c
        e_end = e_inv[rows] * gam_c
        bh_s[rows, :] = kka[rows] * e_end
        kh_s[rows, :] = k2[rows] * e_end
    g_b = jnp.dot(h, win_ref[:, 2 * D_POOL + D_SHIFT:], preferred_element_type=F32)
    gate_b = _silu(g_b)

    ri = lax.broadcasted_iota(jnp.int32, (CHUNK, PAIR), 0)
    ci = lax.broadcasted_iota(jnp.int32, (CHUNK, PAIR), 1)
    cm = ci % CHUNK
    left = ci < CHUNK
    strict_k = (ri > cm) & ~left
    strict_b = (ri > cm) & left
    incl = ri >= cm
    eye_r = jnp.where(ci == ri + CHUNK, 1.0, 0.0)
    lane_half = lax.broadcasted_iota(jnp.int32, (1, PAIR), 1) // HEAD
    key_mask = [lane_half == e for e in range(2)]
    eye_key = [ci == ri + e * HEAD for e in range(2)]
    zeros_blk = jnp.zeros((HEAD, PAIR), F32)
    keep_right = jnp.where(lane_half == 1, 1.0, 0.0)
    n_doubling = int(math.log2(CHUNK))
    heads = [(p, e) for p in range(N_HEADS // 2) for e in range(2)]
    chains = [(c, p, e) for c in range(N_CHUNKS) for p, e in heads]
    blk = lambda ref, c, p: ref[c * CHUNK:(c + 1) * CHUNK, p * PAIR:(p + 1) * PAIR]

    aall, vm, z, ry, tx, mgk = [], [], [], [], [], []
    for c, p, e in chains:
        if e == 0:
            lq = jnp.concatenate([blk(at_s, c, p), blk(rt_s, c, p)], axis=0)
            rk_t = jnp.concatenate([blk(bt_s, c, p), blk(kt_s, c, p)], axis=0).T
            rk_pair = jnp.concatenate(
                [jnp.concatenate([rk_t[:HEAD], zeros_blk], axis=1),
                 jnp.concatenate([zeros_blk, rk_t[HEAD:]], axis=1)], axis=0)
            scores = _mm(lq, rk_pair)
        aall.append(scores[:, e * PAIR:(e + 1) * PAIR])
        vm.append(jnp.where(key_mask[e], 0.0, blk(v_s, c, p)))
    for i in range(len(chains)):
        tx.append(jnp.where(strict_b, aall[i][:CHUNK], eye_r))
    per_level = -(-len(chains) // n_doubling)
    for level in range(n_doubling):
        skip = (2 ** level) // MXU_ROWS * MXU_ROWS
        for i in range(len(chains)):
            upd = _mm(tx[i][skip:, :CHUNK], tx[i])
            if skip:
                upd = jnp.concatenate([jnp.zeros((skip, PAIR), F32), upd], axis=0)
            tx[i] = tx[i] * keep_right + upd
        for i, (c, p, e) in list(enumerate(chains))[level * per_level:(level + 1) * per_level]:
            a_ak = jnp.where(strict_k, aall[i][:CHUNK], 0.0)
            av = _mm(a_ak, jnp.concatenate([vm[i], vm[i]], axis=0))
            z.append(jnp.where(key_mask[e], blk(at_s, c, p), av))
    for i in range(len(chains)):
        z[i] = _mm(tx[i], jnp.concatenate([z[i], z[i]], axis=0))
    for i, (c, p, e) in enumerate(chains):
        a_r = jnp.where(incl, aall[i][CHUNK:], 0.0)
        if e == 0:
            bk_t = jnp.concatenate([blk(bh_s, c, p), blk(kh_s, c, p)], axis=0).T
        both = _mm(jnp.concatenate([a_r, bk_t[e * HEAD:(e + 1) * HEAD]], axis=0),
                   jnp.concatenate([z[i], vm[i]], axis=0))
        ry.append(jnp.where(key_mask[e], blk(rt_s, c, p), 0.0) + both[:CHUNK])
        gam_c = ge_s[c:c + 1, p * PAIR:(p + 1) * PAIR]
        mgk.append(both[CHUNK:] + jnp.where(eye_key[e], gam_c, 0.0))
    y_blocks = []
    for c in range(N_CHUNKS):
        ys = []
        for p in range(N_HEADS // 2):
            i0 = c * N_HEADS + 2 * p
            sp = state[p]
            rhs = jnp.concatenate([jnp.where(key_mask[0], 0.0, sp),
                                   jnp.where(key_mask[0], sp, 0.0)], axis=0)
            ry_pair = jnp.where(key_mask[0], ry[i0], ry[i0 + 1])
            mgk_pair = jnp.where(key_mask[0], mgk[i0], mgk[i0 + 1])
            res = _mm(jnp.concatenate([ry_pair, mgk_pair], axis=0), rhs)
            ys.append(res[:CHUNK] + jnp.where(key_mask[0], ry[i0 + 1], ry[i0]))
            state[p] = res[CHUNK:] + jnp.where(key_mask[0], mgk[i0 + 1], mgk[i0])
        y_blocks.append(jnp.concatenate(ys, axis=1))
    y = jnp.concatenate(y_blocks, axis=0)

    mean = _pair_sum(y, headsum) * (1.0 / HEAD)
    d = y - mean
    var = _pair_sum(d * d, headsum) * (1.0 / HEAD)
    y = d * lax.rsqrt(var + GN_EPS) * gng_ref[...] + gnb_ref[...]
    y_b = (y + bonus) * gate_b

    yy = jnp.concatenate([y_a, y_b], axis=1).astype(BF16)
    xo = x + jnp.dot(yy, wout_ref[...], preferred_element_type=F32)
    inv_o = lax.rsqrt(jnp.mean(xo * xo, axis=-1, keepdims=True) + NORM_EPS)
    o_ref[0, tile_rows, :] = xo * inv_o * fg_ref[...]


def _block_kernel(x_ref, ng_ref, win_f, poolw_f, pscale_ref, mu_f, w0_ref, wup_f,
                  a0_ref, aup_f, kk_ref, ka_ref, rk_ref, gng_f, gnb_f, wout_f, fg_ref,
                  headsum_ref, headswap_ref, cumtri_ref,
                  o_ref,
                  shift_carry, pool_carry, state,
                  at_s, rt_s, bt_s, kt_s, bh_s, kh_s, v_s, ge_s,
                  win_ref, poolw_ref, mu_ref, lora_ref, gng_ref, gnb_ref, wout_ref):
    step = pl.program_id(1)

    @pl.when((pl.program_id(0) == 0) & (step == 0))
    def _():
        _prepare_weights(win_f, poolw_f, mu_f, wup_f, aup_f, gng_f, gnb_f, wout_f,
                         win_ref, poolw_ref, mu_ref, lora_ref, gng_ref, gnb_ref, wout_ref)

    def body(i, carry):
        _tile(step * STEP_TILES + i, pl.ds(pl.multiple_of(i * TILE, TILE), TILE),
              x_ref, ng_ref, pscale_ref, w0_ref, a0_ref, kk_ref, ka_ref, rk_ref, fg_ref,
              headsum_ref, headswap_ref, cumtri_ref, o_ref, shift_carry, pool_carry, state,
              at_s, rt_s, bt_s, kt_s, bh_s, kh_s, v_s, ge_s,
              win_ref, poolw_ref, mu_ref, lora_ref, gng_ref, gnb_ref, wout_ref)
        return carry

    lax.fori_loop(0, STEP_TILES, body, 0)


def _full(shape, single_buffer=False):
    index_map = lambda b, j: (0,) * len(shape)
    if single_buffer:
        return pl.BlockSpec(shape, index_map, pipeline_mode=pl.Buffered(1))
    return pl.BlockSpec(shape, index_map)


def kernel(x, norm_gain, w_in, pool_w, pool_scale, shift_mu, w0, w_up, a0, a_up, k_k, k_a, r_k,
           gn_gain, gn_bias, w_out, final_gain):
    batch, seq, d_model = x.shape
    step_rows = STEP_TILES * TILE
    assert norm_gain.shape[0] == 1 and d_model == D_MODEL and seq % step_rows == 0
    assert w_in.shape == (1, D_MODEL, D_IN)

    hid = jnp.arange(PAIR) // HEAD
    headsum = (hid[:, None] == hid[None, :]).astype(BF16)
    headsum_swap = (hid[:, None] != hid[None, :]).astype(BF16)
    t = jnp.arange(TILE)
    same_chunk = (t[:, None] // CHUNK) == (t[None, :] // CHUNK)
    cumtri = (same_chunk & (t[:, None] >= t[None, :])).astype(BF16)

    operands = (
        x, norm_gain, w_in, pool_w, pool_scale, shift_mu, w0, w_up, a0, a_up, k_k, k_a,
        r_k.reshape(1, D_RWKV), gn_gain, gn_bias, w_out, final_gain.reshape(1, D_MODEL),
        headsum, headsum_swap, cumtri,
    )
    big = {2, 15}
    in_specs = [pl.BlockSpec((1, step_rows, D_MODEL), lambda b, j: (b, j, 0))]
    in_specs += [_full(op.shape, i in big) for i, op in enumerate(operands) if i > 0]
    seq_buf = lambda: pltpu.VMEM((TILE, D_RWKV), F32)
    return pl.pallas_call(
        _block_kernel,
        grid=(batch, seq // step_rows),
        in_specs=in_specs,
        out_specs=pl.BlockSpec((1, step_rows, D_MODEL), lambda b, j: (b, j, 0)),
        out_shape=jax.ShapeDtypeStruct(x.shape, x.dtype),
        scratch_shapes=[
            pltpu.VMEM((1, D_SHIFT), F32),
            pltpu.VMEM((POOL_HALO, D_POOL), F32),
            pltpu.VMEM((N_HEADS // 2, HEAD, PAIR), F32),
        ] + [seq_buf() for _ in range(7)] + [
            pltpu.VMEM((8, D_RWKV), F32),
            pltpu.VMEM((D_MODEL, D_IN), BF16),
            pltpu.VMEM((len(POOL_WINDOWS), POOL_GROUP, POOL_GROUP), BF16),
            pltpu.VMEM((1, D_SHIFT), F32),
            pltpu.VMEM((2 * LORA, 2 * D_RWKV), BF16),
            pltpu.VMEM((1, D_RWKV), F32),
            pltpu.VMEM((1, D_RWKV), F32),
            pltpu.VMEM((D_MODEL, D_MODEL), BF16),
        ],
        compiler_params=pltpu.CompilerParams(
            dimension_semantics=("arbitrary", "arbitrary"),
            vmem_limit_bytes=VMEM_LIMIT_BYTES),
        name="hybrid_pool_rwkv7_block",
    )(*operands)
```

```python
import math

import jax
import jax.numpy as jnp
from jax import lax
from jax.experimental import pallas as pl
from jax.experimental.pallas import tpu as pltpu

F32 = jnp.float32
BF16 = jnp.bfloat16

D_MODEL = 1024
D_POOL = 512
POOL_WINDOWS = (2, 4, 8, 16)
POOL_GROUP = 128
POOL_HALO = 16
D_RWKV = 512
N_HEADS = 8
HEAD = 64
PAIR = 2 * HEAD
LORA = 64
D_SHIFT = 3 * D_RWKV + 2 * LORA
D_IN = 2 * D_POOL + D_SHIFT + D_RWKV
NORM_EPS = 1e-6
GN_EPS = 64e-5
CHUNK = 64
TILE = 256
STEP_TILES = 4
N_CHUNKS = TILE // CHUNK
DECAY_SCALE = math.exp(-0.5)
VMEM_LIMIT_BYTES = 56 * 1024 * 1024


def _mm(a, b):
    return jnp.dot(a.astype(BF16), b.astype(BF16), preferred_element_type=F32)


def _split(a, parts):
    out = []
    rem = a
    for _ in range(parts):
        p = rem.astype(BF16)
        out.append(p)
        rem = rem - p.astype(F32)
    return out


def _mm_exact_lhs(a_bf16, b, parts):
    acc = None
    for p in _split(b, parts):
        t = jnp.dot(a_bf16, p, preferred_element_type=F32)
        acc = t if acc is None else acc + t
    return acc


def _pair_sum(a, ones_blk):
    return jnp.concatenate([_mm(a[:, c:c + PAIR], ones_blk) for c in range(0, a.shape[1], PAIR)],
                           axis=1)


def _sigmoid(t):
    return 0.5 * jnp.tanh(0.5 * t) + 0.5


def _silu(t):
    return t * _sigmoid(t)


def _swap_pair_lanes(a):
    rows = a.shape[0]
    if rows < 8:
        a = jnp.broadcast_to(a[0:1], (8, a.shape[1]))
    out = jnp.concatenate([pltpu.roll(a[:, c:c + PAIR], HEAD, axis=1)
                           for c in range(0, a.shape[1], PAIR)], axis=1)
    return out[:rows]


def _prepare_weights(win_f, poolw_f, mu_f, wup_f, aup_f, gng_f, gnb_f, wout_f,
                     win_ref, poolw_ref, mu_ref, lora_ref, gng_ref, gnb_ref, wout_ref):
    v0 = 2 * D_POOL + 2 * D_RWKV
    g0 = 2 * D_POOL + D_SHIFT
    for c in range(0, D_IN, PAIR):
        w = win_f[0, :, c:c + PAIR]
        if v0 <= c < v0 + D_RWKV or c >= g0:
            w = pltpu.roll(w, HEAD, axis=1)
        win_ref[:, c:c + PAIR] = w.astype(BF16)
    wout_ref[0:D_POOL, :] = wout_f[0, 0:D_POOL, :].astype(BF16)
    for r in range(D_POOL, D_MODEL, PAIR):
        wout_ref[r:r + HEAD, :] = wout_f[0, r + HEAD:r + PAIR, :].astype(BF16)
        wout_ref[r + HEAD:r + PAIR, :] = wout_f[0, r:r + HEAD, :].astype(BF16)
    poolw_ref[...] = poolw_f[0].astype(BF16)
    lora_ref[...] = jnp.zeros_like(lora_ref)
    lora_ref[0:LORA, 0:D_RWKV] = wup_f[0].astype(BF16)
    lora_ref[LORA:2 * LORA, D_RWKV:2 * D_RWKV] = aup_f[0].astype(BF16)
    mu_ref[...] = mu_f[...]
    mu_ref[:, 2 * D_RWKV:3 * D_RWKV] = _swap_pair_lanes(mu_f[:, 2 * D_RWKV:3 * D_RWKV])
    gng_ref[...] = _swap_pair_lanes(gng_f[...])
    gnb_ref[...] = _swap_pair_lanes(gnb_f[...])


def _tile(j, tile_rows, x_ref, ng_ref, pscale_ref, w0_ref, a0_ref, kk_ref, ka_ref, rk_ref, fg_ref,
          headsum_ref, headswap_ref, cumtri_ref, o_ref, shift_carry, pool_carry, state,
          at_s, rt_s, bt_s, kt_s, bh_s, kh_s, v_s, ge_s,
          win_ref, poolw_ref, mu_ref, lora_ref, gng_ref, gnb_ref, wout_ref):
    @pl.when(j == 0)
    def _():
        shift_carry[...] = jnp.zeros_like(shift_carry)
        pool_carry[...] = jnp.zeros_like(pool_carry)
        state[...] = jnp.zeros_like(state)

    x = x_ref[0, tile_rows, :]
    inv = lax.rsqrt(jnp.mean(x * x, axis=-1, keepdims=True) + NORM_EPS)
    h = (x * inv * ng_ref[...]).astype(BF16)

    row = lax.broadcasted_iota(jnp.int32, (TILE, 1), 0)

    z_sh = jnp.dot(h, win_ref[:, 2 * D_POOL:2 * D_POOL + D_SHIFT], preferred_element_type=F32)
    prev = pltpu.roll(z_sh, 1, axis=0)
    head_rows = jnp.where(row[:8] == 0, shift_carry[...], prev[:8])
    prev = jnp.concatenate([head_rows, prev[8:]], axis=0)
    shift_carry[...] = z_sh[TILE - 1:TILE, :]
    sh = z_sh + mu_ref[...] * (prev - z_sh)

    z_pool = jnp.dot(h, win_ref[:, 0:2 * D_POOL], preferred_element_type=F32)
    u = z_pool[:, :D_POOL]
    g_a = z_pool[:, D_POOL:]
    ext = jnp.concatenate([pool_carry[...], u], axis=0)
    pool_carry[...] = u[TILE - POOL_HALO:, :]
    pos = (j * TILE + row).astype(F32)
    pooled = []
    s = ext
    for g, w in enumerate(POOL_WINDOWS):
        lo = g * POOL_GROUP
        s = s[:, (POOL_GROUP if g > 0 else 0):]
        s = s + pltpu.roll(s, w // 2, axis=0)
        inv_cnt = 1.0 / jnp.minimum(pos + 1.0, float(w))
        pooled.append(s[POOL_HALO:, :POOL_GROUP] * inv_cnt - u[:, lo:lo + POOL_GROUP])
    mixed = [_mm(pooled[g], poolw_ref[g]) for g in range(len(POOL_WINDOWS))]
    y_a = jnp.concatenate(mixed, axis=1) * pscale_ref[...] * _silu(g_a)

    r = sh[:, 0:D_RWKV]
    k = sh[:, D_RWKV:2 * D_RWKV]
    v = sh[:, 2 * D_RWKV:3 * D_RWKV]
    lr = sh[:, 3 * D_RWKV:]
    lane = lax.broadcasted_iota(jnp.int32, (1, 2 * LORA), 1)
    lr = jnp.where(lane < LORA, jnp.tanh(lr), lr)
    lora = _mm(lr, lora_ref[...])
    ww = w0_ref[...] + lora[:, :D_RWKV]
    a = _sigmoid(a0_ref[...] + lora[:, D_RWKV:])
    logw = -DECAY_SCALE * _sigmoid(ww)

    headsum = headsum_ref[...]
    kkv = k * kk_ref[...]
    ss = _pair_sum(kkv * kkv, headsum)
    kkn = kkv * lax.rsqrt(jnp.maximum(ss, 1e-24))
    k2 = k * (1.0 + (a - 1.0) * ka_ref[...])
    bonus = _pair_sum(r * k2 * rk_ref[...], headswap_ref[...]) * v

    gcum = _mm_exact_lhs(cumtri_ref[...], logw, 2)
    e_inv = jnp.exp(-gcum)
    kka = kkn * a
    at_s[...] = -kkn * jnp.exp(gcum - logw)
    rt_s[...] = r * jnp.exp(gcum)
    bt_s[...] = kka * e_inv
    kt_s[...] = k2 * e_inv
    v_s[...] = v
    for c in range(N_CHUNKS):
        rows = slice(c * CHUNK, (c + 1) * CHUNK)
        gam_c = jnp.exp(gcum[rows.stop - 1:rows.stop, :])
        ge_s[c:c + 1, :] = gam_c
        e_end = e_inv[rows] * gam_c
        bh_s[rows, :] = kka[rows] * e_end
        kh_s[rows, :] = k2[rows] * e_end
    g_b = jnp.dot(h, win_ref[:, 2 * D_POOL + D_SHIFT:], preferred_element_type=F32)
    gate_b = _silu(g_b)

    ri = lax.broadcasted_iota(jnp.int32, (CHUNK, PAIR), 0)
    ci = lax.broadcasted_iota(jnp.int32, (CHUNK, PAIR), 1)
    cm = ci % CHUNK
    left = ci < CHUNK
    strict_k = (ri > cm) & ~left
    strict_b = (ri > cm) & left
    incl = ri >= cm
    eye_r = jnp.where(ci == ri + CHUNK, 1.0, 0.0)
    lane_half = lax.broadcasted_iota(jnp.int32, (1, PAIR), 1) // HEAD
    key_mask = [lane_half == e for e in range(2)]
    eye_key = [ci == ri + e * HEAD for e in range(2)]
    zeros_blk = jnp.zeros((HEAD, PAIR), F32)
    keep_right = jnp.where(lane_half == 1, 1.0, 0.0)
    n_doubling = int(math.log2(CHUNK))
    heads = [(p, e) for p in range(N_HEADS // 2) for e in range(2)]
    chains = [(c, p, e) for c in range(N_CHUNKS) for p, e in heads]
    blk = lambda ref, c, p: ref[c * CHUNK:(c + 1) * CHUNK, p * PAIR:(p + 1) * PAIR]

    aall, vm, z, ry, tx, mgk = [], [], [], [], [], []
    for c, p, e in chains:
        if e == 0:
            lq = jnp.concatenate([blk(at_s, c, p), blk(rt_s, c, p)], axis=0)
            rk_t = jnp.concatenate([blk(bt_s, c, p), blk(kt_s, c, p)], axis=0).T
            rk_pair = jnp.concatenate(
                [jnp.concatenate([rk_t[:HEAD], zeros_blk], axis=1),
                 jnp.concatenate([zeros_blk, rk_t[HEAD:]], axis=1)], axis=0)
            scores = _mm(lq, rk_pair)
        aall.append(scores[:, e * PAIR:(e + 1) * PAIR])
        vm.append(jnp.where(key_mask[e], 0.0, blk(v_s, c, p)))
    for i in range(len(chains)):
        tx.append(jnp.where(strict_b, aall[i][:CHUNK], eye_r))
    per_level = -(-len(chains) // n_doubling)
    for level in range(n_doubling):
        for i in range(len(chains)):
            tx[i] = tx[i] * keep_right + _mm(tx[i][:, :CHUNK], tx[i])
        for i, (c, p, e) in list(enumerate(chains))[level * per_level:(level + 1) * per_level]:
            a_ak = jnp.where(strict_k, aall[i][:CHUNK], 0.0)
            av = _mm(a_ak, jnp.concatenate([vm[i], vm[i]], axis=0))
            z.append(jnp.where(key_mask[e], blk(at_s, c, p), av))
    for i in range(len(chains)):
        z[i] = _mm(tx[i], jnp.concatenate([z[i], z[i]], axis=0))
    for i, (c, p, e) in enumerate(chains):
        a_r = jnp.where(incl, aall[i][CHUNK:], 0.0)
        if e == 0:
            bk_t = jnp.concatenate([blk(bh_s, c, p), blk(kh_s, c, p)], axis=0).T
        both = _mm(jnp.concatenate([a_r, bk_t[e * HEAD:(e + 1) * HEAD]], axis=0),
                   jnp.concatenate([z[i], vm[i]], axis=0))
        ry.append(jnp.where(key_mask[e], blk(rt_s, c, p), 0.0) + both[:CHUNK])
        gam_c = ge_s[c:c + 1, p * PAIR:(p + 1) * PAIR]
        mgk.append(both[CHUNK:] + jnp.where(eye_key[e], gam_c, 0.0))
    y_blocks = []
    for c in range(N_CHUNKS):
        ys = []
        for p in range(N_HEADS // 2):
            i0 = c * N_HEADS + 2 * p
            sp = state[p]
            rhs = jnp.concatenate([jnp.where(key_mask[0], 0.0, sp),
                                   jnp.where(key_mask[0], sp, 0.0)], axis=0)
            ry_pair = jnp.where(key_mask[0], ry[i0], ry[i0 + 1])
            mgk_pair = jnp.where(key_mask[0], mgk[i0], mgk[i0 + 1])
            res = _mm(jnp.concatenate([ry_pair, mgk_pair], axis=0), rhs)
            ys.append(res[:CHUNK] + jnp.where(key_mask[0], ry[i0 + 1], ry[i0]))
            state[p] = res[CHUNK:] + jnp.where(key_mask[0], mgk[i0 + 1], mgk[i0])
        y_blocks.append(jnp.concatenate(ys, axis=1))
    y = jnp.concatenate(y_blocks, axis=0)

    mean = _pair_sum(y, headsum) * (1.0 / HEAD)
    d = y - mean
    var = _pair_sum(d * d, headsum) * (1.0 / HEAD)
    y = d * lax.rsqrt(var + GN_EPS) * gng_ref[...] + gnb_ref[...]
    y_b = (y + bonus) * gate_b

    yy = jnp.concatenate([y_a, y_b], axis=1).astype(BF16)
    xo = x + jnp.dot(yy, wout_ref[...], preferred_element_type=F32)
    inv_o = lax.rsqrt(jnp.mean(xo * xo, axis=-1, keepdims=True) + NORM_EPS)
    o_ref[0, tile_rows, :] = xo * inv_o * fg_ref[...]


def _block_kernel(x_ref, ng_ref, win_f, poolw_f, pscale_ref, mu_f, w0_ref, wup_f,
                  a0_ref, aup_f, kk_ref, ka_ref, rk_ref, gng_f, gnb_f, wout_f, fg_ref,
                  headsum_ref, headswap_ref, cumtri_ref,
                  o_ref,
                  shift_carry, pool_carry, state,
                  at_s, rt_s, bt_s, kt_s, bh_s, kh_s, v_s, ge_s,
                  win_ref, poolw_ref, mu_ref, lora_ref, gng_ref, gnb_ref, wout_ref):
    step = pl.program_id(1)

    @pl.when((pl.program_id(0) == 0) & (step == 0))
    def _():
        _prepare_weights(win_f, poolw_f, mu_f, wup_f, aup_f, gng_f, gnb_f, wout_f,
                         win_ref, poolw_ref, mu_ref, lora_ref, gng_ref, gnb_ref, wout_ref)

    def body(i, carry):
        _tile(step * STEP_TILES + i, pl.ds(pl.multiple_of(i * TILE, TILE), TILE),
              x_ref, ng_ref, pscale_ref, w0_ref, a0_ref, kk_ref, ka_ref, rk_ref, fg_ref,
              headsum_ref, headswap_ref, cumtri_ref, o_ref, shift_carry, pool_carry, state,
              at_s, rt_s, bt_s, kt_s, bh_s, kh_s, v_s, ge_s,
              win_ref, poolw_ref, mu_ref, lora_ref, gng_ref, gnb_ref, wout_ref)
        return carry

    lax.fori_loop(0, STEP_TILES, body, 0)


def _full(shape, single_buffer=False):
    index_map = lambda b, j: (0,) * len(shape)
    if single_buffer:
        return pl.BlockSpec(shape, index_map, pipeline_mode=pl.Buffered(1))
    return pl.BlockSpec(shape, index_map)


def kernel(x, norm_gain, w_in, pool_w, pool_scale, shift_mu, w0, w_up, a0, a_up, k_k, k_a, r_k,
           gn_gain, gn_bias, w_out, final_gain):
    batch, seq, d_model = x.shape
    step_rows = STEP_TILES * TILE
    assert norm_gain.shape[0] == 1 and d_model == D_MODEL and seq % step_rows == 0
    assert w_in.shape == (1, D_MODEL, D_IN)

    hid = jnp.arange(PAIR) // HEAD
    headsum = (hid[:, None] == hid[None, :]).astype(BF16)
    headsum_swap = (hid[:, None] != hid[None, :]).astype(BF16)
    t = jnp.arange(TILE)
    same_chunk = (t[:, None] // CHUNK) == (t[None, :] // CHUNK)
    cumtri = (same_chunk & (t[:, None] >= t[None, :])).astype(BF16)

    operands = (
        x, norm_gain, w_in, pool_w, pool_scale, shift_mu, w0, w_up, a0, a_up, k_k, k_a,
        r_k.reshape(1, D_RWKV), gn_gain, gn_bias, w_out, final_gain.reshape(1, D_MODEL),
        headsum, headsum_swap, cumtri,
    )
    in_specs = [pl.BlockSpec((1, step_rows, D_MODEL), lambda b, j: (b, j, 0))]
    in_specs += [_full(op.shape, single_buffer=op is w_in or op is w_out) for op in operands[1:]]
    seq_buf = lambda: pltpu.VMEM((TILE, D_RWKV), F32)
    return pl.pallas_call(
        _block_kernel,
        grid=(batch, seq // step_rows),
        in_specs=in_specs,
        out_specs=pl.BlockSpec((1, step_rows, D_MODEL), lambda b, j: (b, j, 0)),
        out_shape=jax.ShapeDtypeStruct(x.shape, x.dtype),
        scratch_shapes=[
            pltpu.VMEM((1, D_SHIFT), F32),
            pltpu.VMEM((POOL_HALO, D_POOL), F32),
            pltpu.VMEM((N_HEADS // 2, HEAD, PAIR), F32),
        ] + [seq_buf() for _ in range(7)] + [
            pltpu.VMEM((8, D_RWKV), F32),
            pltpu.VMEM((D_MODEL, D_IN), BF16),
            pltpu.VMEM((len(POOL_WINDOWS), POOL_GROUP, POOL_GROUP), BF16),
            pltpu.VMEM((1, D_SHIFT), F32),
            pltpu.VMEM((2 * LORA, 2 * D_RWKV), BF16),
            pltpu.VMEM((1, D_RWKV), F32),
            pltpu.VMEM((1, D_RWKV), F32),
            pltpu.VMEM((D_MODEL, D_MODEL), BF16),
        ],
        compiler_params=pltpu.CompilerParams(
            dimension_semantics=("arbitrary", "arbitrary"),
            vmem_limit_bytes=VMEM_LIMIT_BYTES),
        name="hybrid_pool_rwkv7_block",
    )(*operands)
```

```python
import math

import jax
import jax.numpy as jnp
from jax import lax
from jax.experimental import pallas as pl
from jax.experimental.pallas import tpu as pltpu

F32 = jnp.float32
BF16 = jnp.bfloat16

D_MODEL = 1024
D_POOL = 512
POOL_WINDOWS = (2, 4, 8, 16)
POOL_GROUP = 128
POOL_HALO = 16
D_RWKV = 512
N_HEADS = 8
HEAD = 64
PAIR = 2 * HEAD
LORA = 64
D_SHIFT = 3 * D_RWKV + 2 * LORA
D_IN = 2 * D_POOL + D_SHIFT + D_RWKV
NORM_EPS = 1e-6
GN_EPS = 64e-5
CHUNK = 64
TILE = 256
STEP_TILES = 4
N_CHUNKS = TILE // CHUNK
DECAY_SCALE = math.exp(-0.5)
VMEM_LIMIT_BYTES = 56 * 1024 * 1024


def _mm(a, b):
    return jnp.dot(a.astype(BF16), b.astype(BF16), preferred_element_type=F32)


def _split(a, parts):
    out = []
    rem = a
    for _ in range(parts):
        p = rem.astype(BF16)
        out.append(p)
        rem = rem - p.astype(F32)
    return out


def _mm_exact_lhs(a_bf16, b, parts):
    acc = None
    for p in _split(b, parts):
        t = jnp.dot(a_bf16, p, preferred_element_type=F32)
        acc = t if acc is None else acc + t
    return acc


def _pair_sum(a, ones_blk):
    return jnp.concatenate([_mm(a[:, c:c + PAIR], ones_blk) for c in range(0, a.shape[1], PAIR)],
                           axis=1)


def _sigmoid(t):
    return 0.5 * jnp.tanh(0.5 * t) + 0.5


def _silu(t):
    return t * _sigmoid(t)


def _swap_pair_lanes(a):
    rows = a.shape[0]
    if rows < 8:
        a = jnp.broadcast_to(a[0:1], (8, a.shape[1]))
    out = jnp.concatenate([pltpu.roll(a[:, c:c + PAIR], HEAD, axis=1)
                           for c in range(0, a.shape[1], PAIR)], axis=1)
    return out[:rows]


def _prepare_weights(win_f, poolw_f, mu_f, wup_f, aup_f, gng_f, gnb_f, wout_f,
                     win_ref, poolw_ref, mu_ref, lora_ref, gng_ref, gnb_ref, wout_ref):
    v0 = 2 * D_POOL + 2 * D_RWKV
    g0 = 2 * D_POOL + D_SHIFT
    for c in range(0, D_IN, PAIR):
        w = win_f[0, :, c:c + PAIR]
        if v0 <= c < v0 + D_RWKV or c >= g0:
            w = pltpu.roll(w, HEAD, axis=1)
        win_ref[:, c:c + PAIR] = w.astype(BF16)
    wout_ref[0:D_POOL, :] = wout_f[0, 0:D_POOL, :].astype(BF16)
    for r in range(D_POOL, D_MODEL, PAIR):
        wout_ref[r:r + HEAD, :] = wout_f[0, r + HEAD:r + PAIR, :].astype(BF16)
        wout_ref[r + HEAD:r + PAIR, :] = wout_f[0, r:r + HEAD, :].astype(BF16)
    poolw_ref[...] = poolw_f[0].astype(BF16)
    lora_ref[...] = jnp.zeros_like(lora_ref)
    lora_ref[0:LORA, 0:D_RWKV] = wup_f[0].astype(BF16)
    lora_ref[LORA:2 * LORA, D_RWKV:2 * D_RWKV] = aup_f[0].astype(BF16)
    mu_ref[...] = mu_f[...]
    mu_ref[:, 2 * D_RWKV:3 * D_RWKV] = _swap_pair_lanes(mu_f[:, 2 * D_RWKV:3 * D_RWKV])
    gng_ref[...] = _swap_pair_lanes(gng_f[...])
    gnb_ref[...] = _swap_pair_lanes(gnb_f[...])


def _tile(j, tile_rows, x_ref, ng_ref, pscale_ref, w0_ref, a0_ref, kk_ref, ka_ref, rk_ref, fg_ref,
          headsum_ref, headswap_ref, cumtri_ref, o_ref, shift_carry, pool_carry, state,
          at_s, rt_s, bt_s, kt_s, bh_s, kh_s, v_s, ge_s,
          win_ref, poolw_ref, mu_ref, lora_ref, gng_ref, gnb_ref, wout_ref):
    @pl.when(j == 0)
    def _():
        shift_carry[...] = jnp.zeros_like(shift_carry)
        pool_carry[...] = jnp.zeros_like(pool_carry)
        state[...] = jnp.zeros_like(state)

    x = x_ref[0, tile_rows, :]
    inv = lax.rsqrt(jnp.mean(x * x, axis=-1, keepdims=True) + NORM_EPS)
    h = (x * inv * ng_ref[...]).astype(BF16)

    row = lax.broadcasted_iota(jnp.int32, (TILE, 1), 0)

    z_sh = jnp.dot(h, win_ref[:, 2 * D_POOL:2 * D_POOL + D_SHIFT], preferred_element_type=F32)
    prev = pltpu.roll(z_sh, 1, axis=0)
    head_rows = jnp.where(row[:8] == 0, shift_carry[...], prev[:8])
    prev = jnp.concatenate([head_rows, prev[8:]], axis=0)
    shift_carry[...] = z_sh[TILE - 1:TILE, :]
    sh = z_sh + mu_ref[...] * (prev - z_sh)

    z_pool = jnp.dot(h, win_ref[:, 0:2 * D_POOL], preferred_element_type=F32)
    u = z_pool[:, :D_POOL]
    g_a = z_pool[:, D_POOL:]
    ext = jnp.concatenate([pool_carry[...], u], axis=0)
    pool_carry[...] = u[TILE - POOL_HALO:, :]
    pos = (j * TILE + row).astype(F32)
    pooled = []
    s = ext
    for g, w in enumerate(POOL_WINDOWS):
        lo = g * POOL_GROUP
        s = s[:, (POOL_GROUP if g > 0 else 0):]
        s = s + pltpu.roll(s, w // 2, axis=0)
        inv_cnt = 1.0 / jnp.minimum(pos + 1.0, float(w))
        pooled.append(s[POOL_HALO:, :POOL_GROUP] * inv_cnt - u[:, lo:lo + POOL_GROUP])
    mixed = [_mm(pooled[g], poolw_ref[g]) for g in range(len(POOL_WINDOWS))]
    y_a = jnp.concatenate(mixed, axis=1) * pscale_ref[...] * _silu(g_a)

    lr = sh[:, 3 * D_RWKV:]
    lane = lax.broadcasted_iota(jnp.int32, (1, 2 * LORA), 1)
    lr = jnp.where(lane < LORA, jnp.tanh(lr), lr)
    lora = _mm(lr, lora_ref[...])
    headsum = headsum_ref[...]
    bonus_parts = []
    for p in range(N_HEADS // 2):
        ln = slice(p * PAIR, (p + 1) * PAIR)
        off = lambda base: slice(base + p * PAIR, base + (p + 1) * PAIR)
        r = sh[:, ln]
        k = sh[:, off(D_RWKV)]
        v = sh[:, off(2 * D_RWKV)]
        ww = w0_ref[:, ln] + lora[:, ln]
        a = _sigmoid(a0_ref[:, ln] + lora[:, off(D_RWKV)])
        logw = -DECAY_SCALE * _sigmoid(ww)

        kkv = k * kk_ref[:, ln]
        ss = _mm(kkv * kkv, headsum)
        kkn = kkv * lax.rsqrt(jnp.maximum(ss, 1e-24))
        k2 = k * (1.0 + (a - 1.0) * ka_ref[:, ln])
        bonus_parts.append(_mm(r * k2 * rk_ref[:, ln], headswap_ref[...]) * v)

        gcum = jnp.concatenate(
            [_mm_exact_lhs(cumtri_ref[...], logw[hb:hb + PAIR], 2) for hb in range(0, TILE, PAIR)],
            axis=0)
        e_inv = jnp.exp(-gcum)
        kka = kkn * a
        at_s[:, ln] = -kkn * jnp.exp(gcum - logw)
        rt_s[:, ln] = r * jnp.exp(gcum)
        bt_s[:, ln] = kka * e_inv
        kt_s[:, ln] = k2 * e_inv
        v_s[:, ln] = v
        for c in range(N_CHUNKS):
            rows = slice(c * CHUNK, (c + 1) * CHUNK)
            gam_c = jnp.exp(gcum[rows.stop - 1:rows.stop, :])
            ge_s[c:c + 1, ln] = gam_c
            e_end = e_inv[rows] * gam_c
            bh_s[rows, ln] = kka[rows] * e_end
            kh_s[rows, ln] = k2[rows] * e_end
    bonus = jnp.concatenate(bonus_parts, axis=1)
    g_b = jnp.dot(h, win_ref[:, 2 * D_POOL + D_SHIFT:], preferred_element_type=F32)
    gate_b = _silu(g_b)

    ri = lax.broadcasted_iota(jnp.int32, (CHUNK, PAIR), 0)
    ci = lax.broadcasted_iota(jnp.int32, (CHUNK, PAIR), 1)
    cm = ci % CHUNK
    left = ci < CHUNK
    strict_k = (ri > cm) & ~left
    strict_b = (ri > cm) & left
    incl = ri >= cm
    eye_r = jnp.where(ci == ri + CHUNK, 1.0, 0.0)
    lane_half = lax.broadcasted_iota(jnp.int32, (1, PAIR), 1) // HEAD
    key_mask = [lane_half == e for e in range(2)]
    eye_key = [ci == ri + e * HEAD for e in range(2)]
    zeros_blk = jnp.zeros((HEAD, PAIR), F32)
    keep_right = jnp.where(lane_half == 1, 1.0, 0.0)
    n_doubling = int(math.log2(CHUNK))
    heads = [(p, e) for p in range(N_HEADS // 2) for e in range(2)]
    chains = [(c, p, e) for c in range(N_CHUNKS) for p, e in heads]
    blk = lambda ref, c, p: ref[c * CHUNK:(c + 1) * CHUNK, p * PAIR:(p + 1) * PAIR]

    aall, vm, z, ry, tx, mgk = [], [], [], [], [], []
    for c, p, e in chains:
        if e == 0:
            lq = jnp.concatenate([blk(at_s, c, p), blk(rt_s, c, p)], axis=0)
            rk_t = jnp.concatenate([blk(bt_s, c, p), blk(kt_s, c, p)], axis=0).T
            rk_pair = jnp.concatenate(
                [jnp.concatenate([rk_t[:HEAD], zeros_blk], axis=1),
                 jnp.concatenate([zeros_blk, rk_t[HEAD:]], axis=1)], axis=0)
            scores = _mm(lq, rk_pair)
        aall.append(scores[:, e * PAIR:(e + 1) * PAIR])
        vm.append(jnp.where(key_mask[e], 0.0, blk(v_s, c, p)))
    for i in range(len(chains)):
        tx.append(jnp.where(strict_b, aall[i][:CHUNK], eye_r))
    per_level = -(-len(chains) // n_doubling)
    for level in range(n_doubling):
        for i in range(len(chains)):
            tx[i] = tx[i] * keep_right + _mm(tx[i][:, :CHUNK], tx[i])
        for i, (c, p, e) in list(enumerate(chains))[level * per_level:(level + 1) * per_level]:
            a_ak = jnp.where(strict_k, aall[i][:CHUNK], 0.0)
            av = _mm(a_ak, jnp.concatenate([vm[i], vm[i]], axis=0))
            z.append(jnp.where(key_mask[e], blk(at_s, c, p), av))
    for i in range(len(chains)):
        z[i] = _mm(tx[i], jnp.concatenate([z[i], z[i]], axis=0))
    for i, (c, p, e) in enumerate(chains):
        a_r = jnp.where(incl, aall[i][CHUNK:], 0.0)
        if e == 0:
            bk_t = jnp.concatenate([blk(bh_s, c, p), blk(kh_s, c, p)], axis=0).T
        both = _mm(jnp.concatenate([a_r, bk_t[e * HEAD:(e + 1) * HEAD]], axis=0),
                   jnp.concatenate([z[i], vm[i]], axis=0))
        ry.append(jnp.where(key_mask[e], blk(rt_s, c, p), 0.0) + both[:CHUNK])
        gam_c = ge_s[c:c + 1, p * PAIR:(p + 1) * PAIR]
        mgk.append(both[CHUNK:] + jnp.where(eye_key[e], gam_c, 0.0))
    y_blocks = []
    for c in range(N_CHUNKS):
        ys = []
        for p in range(N_HEADS // 2):
            i0 = c * N_HEADS + 2 * p
            sp = state[p]
            rhs = jnp.concatenate([jnp.where(key_mask[0], 0.0, sp),
                                   jnp.where(key_mask[0], sp, 0.0)], axis=0)
            ry_pair = jnp.where(key_mask[0], ry[i0], ry[i0 + 1])
            mgk_pair = jnp.where(key_mask[0], mgk[i0], mgk[i0 + 1])
            res = _mm(jnp.concatenate([ry_pair, mgk_pair], axis=0), rhs)
            ys.append(res[:CHUNK] + jnp.where(key_mask[0], ry[i0 + 1], ry[i0]))
            state[p] = res[CHUNK:] + jnp.where(key_mask[0], mgk[i0 + 1], mgk[i0])
        y_blocks.append(jnp.concatenate(ys, axis=1))
    y = jnp.concatenate(y_blocks, axis=0)

    mean = _pair_sum(y, headsum) * (1.0 / HEAD)
    d = y - mean
    var = _pair_sum(d * d, headsum) * (1.0 / HEAD)
    y = d * lax.rsqrt(var + GN_EPS) * gng_ref[...] + gnb_ref[...]
    y_b = (y + bonus) * gate_b

    yy = jnp.concatenate([y_a, y_b], axis=1).astype(BF16)
    xo = x + jnp.dot(yy, wout_ref[...], preferred_element_type=F32)
    inv_o = lax.rsqrt(jnp.mean(xo * xo, axis=-1, keepdims=True) + NORM_EPS)
    o_ref[0, tile_rows, :] = xo * inv_o * fg_ref[...]


def _block_kernel(x_ref, ng_ref, win_f, poolw_f, pscale_ref, mu_f, w0_ref, wup_f,
                  a0_ref, aup_f, kk_ref, ka_ref, rk_ref, gng_f, gnb_f, wout_f, fg_ref,
                  headsum_ref, headswap_ref, cumtri_ref,
                  o_ref,
                  shift_carry, pool_carry, state,
                  at_s, rt_s, bt_s, kt_s, bh_s, kh_s, v_s, ge_s,
                  win_ref, poolw_ref, mu_ref, lora_ref, gng_ref, gnb_ref, wout_ref):
    step = pl.program_id(1)

    @pl.when((pl.program_id(0) == 0) & (step == 0))
    def _():
        _prepare_weights(win_f, poolw_f, mu_f, wup_f, aup_f, gng_f, gnb_f, wout_f,
                         win_ref, poolw_ref, mu_ref, lora_ref, gng_ref, gnb_ref, wout_ref)

    def body(i, carry):
        _tile(step * STEP_TILES + i, pl.ds(pl.multiple_of(i * TILE, TILE), TILE),
              x_ref, ng_ref, pscale_ref, w0_ref, a0_ref, kk_ref, ka_ref, rk_ref, fg_ref,
              headsum_ref, headswap_ref, cumtri_ref, o_ref, shift_carry, pool_carry, state,
              at_s, rt_s, bt_s, kt_s, bh_s, kh_s, v_s, ge_s,
              win_ref, poolw_ref, mu_ref, lora_ref, gng_ref, gnb_ref, wout_ref)
        return carry

    lax.fori_loop(0, STEP_TILES, body, 0)


def _full(shape, single_buffer=False):
    index_map = lambda b, j: (0,) * len(shape)
    if single_buffer:
        return pl.BlockSpec(shape, index_map, pipeline_mode=pl.Buffered(1))
    return pl.BlockSpec(shape, index_map)


def kernel(x, norm_gain, w_in, pool_w, pool_scale, shift_mu, w0, w_up, a0, a_up, k_k, k_a, r_k,
           gn_gain, gn_bias, w_out, final_gain):
    batch, seq, d_model = x.shape
    step_rows = STEP_TILES * TILE
    assert norm_gain.shape[0] == 1 and d_model == D_MODEL and seq % step_rows == 0
    assert w_in.shape == (1, D_MODEL, D_IN)

    hid = jnp.arange(PAIR) // HEAD
    headsum = (hid[:, None] == hid[None, :]).astype(BF16)
    headsum_swap = (hid[:, None] != hid[None, :]).astype(BF16)
    t = jnp.arange(PAIR)
    same_chunk = (t[:, None] // CHUNK) == (t[None, :] // CHUNK)
    cumtri = (same_chunk & (t[:, None] >= t[None, :])).astype(BF16)

    operands = (
        x, norm_gain, w_in, pool_w, pool_scale, shift_mu, w0, w_up, a0, a_up, k_k, k_a,
        r_k.reshape(1, D_RWKV), gn_gain, gn_bias, w_out, final_gain.reshape(1, D_MODEL),
        headsum, headsum_swap, cumtri,
    )
    in_specs = [pl.BlockSpec((1, step_rows, D_MODEL), lambda b, j: (b, j, 0))]
    in_specs += [_full(op.shape, single_buffer=op is w_in or op is w_out) for op in operands[1:]]
    seq_buf = lambda: pltpu.VMEM((TILE, D_RWKV), F32)
    return pl.pallas_call(
        _block_kernel,
        grid=(batch, seq // step_rows),
        in_specs=in_specs,
        out_specs=pl.BlockSpec((1, step_rows, D_MODEL), lambda b, j: (b, j, 0)),
        out_shape=jax.ShapeDtypeStruct(x.shape, x.dtype),
        scratch_shapes=[
            pltpu.VMEM((1, D_SHIFT), F32),
            pltpu.VMEM((POOL_HALO, D_POOL), F32),
            pltpu.VMEM((N_HEADS // 2, HEAD, PAIR), F32),
        ] + [seq_buf() for _ in range(7)] + [
            pltpu.VMEM((8, D_RWKV), F32),
            pltpu.VMEM((D_MODEL, D_IN), BF16),
            pltpu.VMEM((len(POOL_WINDOWS), POOL_GROUP, POOL_GROUP), BF16),
            pltpu.VMEM((1, D_SHIFT), F32),
            pltpu.VMEM((2 * LORA, 2 * D_RWKV), BF16),
            pltpu.VMEM((1, D_RWKV), F32),
            pltpu.VMEM((1, D_RWKV), F32),
            pltpu.VMEM((D_MODEL, D_MODEL), BF16),
        ],
        compiler_params=pltpu.CompilerParams(
            dimension_semantics=("arbitrary", "arbitrary"),
            vmem_limit_bytes=VMEM_LIMIT_BYTES),
        name="hybrid_pool_rwkv7_block",
    )(*operands)
```

```python
import math

import jax
import jax.numpy as jnp
from jax import lax
from jax.experimental import pallas as pl
from jax.experimental.pallas import tpu as pltpu

F32 = jnp.float32
BF16 = jnp.bfloat16

D_MODEL = 1024
D_POOL = 512
POOL_WINDOWS = (2, 4, 8, 16)
POOL_GROUP = 128
POOL_HALO = 16
D_RWKV = 512
N_HEADS = 8
HEAD = 64
PAIR = 2 * HEAD
LORA = 64
D_SHIFT = 3 * D_RWKV + 2 * LORA
D_IN = 2 * D_POOL + D_SHIFT + D_RWKV
NORM_EPS = 1e-6
GN_EPS = 64e-5
CHUNK = 64
TILE = 256
STEP_TILES = 4
N_CHUNKS = TILE // CHUNK
DECAY_SCALE = math.exp(-0.5)
VMEM_LIMIT_BYTES = 56 * 1024 * 1024


def _mm(a, b):
    return jnp.dot(a.astype(BF16), b.astype(BF16), preferred_element_type=F32)


def _split(a, parts):
    out = []
    rem = a
    for _ in range(parts):
        p = rem.astype(BF16)
        out.append(p)
        rem = rem - p.astype(F32)
    return out


def _mm_exact_lhs(a_bf16, b, parts):
    acc = None
    for p in _split(b, parts):
        t = jnp.dot(a_bf16, p, preferred_element_type=F32)
        acc = t if acc is None else acc + t
    return acc


def _sigmoid(t):
    return 0.5 * jnp.tanh(0.5 * t) + 0.5


def _silu(t):
    return t * _sigmoid(t)


def _swap_pair_lanes(a):
    rows = a.shape[0]
    if rows < 8:
        a = jnp.broadcast_to(a[0:1], (8, a.shape[1]))
    out = jnp.concatenate([pltpu.roll(a[:, c:c + PAIR], HEAD, axis=1)
                           for c in range(0, a.shape[1], PAIR)], axis=1)
    return out[:rows]


def _prepare_weights(win_f, poolw_f, mu_f, wup_f, aup_f, gng_f, gnb_f, wout_f,
                     win_ref, poolw_ref, mu_ref, lora_ref, gng_ref, gnb_ref, wout_ref):
    v0 = 2 * D_POOL + 2 * D_RWKV
    g0 = 2 * D_POOL + D_SHIFT
    for c in range(0, D_IN, PAIR):
        w = win_f[0, :, c:c + PAIR]
        if v0 <= c < v0 + D_RWKV or c >= g0:
            w = pltpu.roll(w, HEAD, axis=1)
        win_ref[:, c:c + PAIR] = w.astype(BF16)
    wout_ref[0:D_POOL, :] = wout_f[0, 0:D_POOL, :].astype(BF16)
    for r in range(D_POOL, D_MODEL, PAIR):
        wout_ref[r:r + HEAD, :] = wout_f[0, r + HEAD:r + PAIR, :].astype(BF16)
        wout_ref[r + HEAD:r + PAIR, :] = wout_f[0, r:r + HEAD, :].astype(BF16)
    poolw_ref[...] = poolw_f[0].astype(BF16)
    lora_ref[...] = jnp.zeros_like(lora_ref)
    lora_ref[0:LORA, 0:D_RWKV] = wup_f[0].astype(BF16)
    lora_ref[LORA:2 * LORA, D_RWKV:2 * D_RWKV] = aup_f[0].astype(BF16)
    mu_ref[...] = mu_f[...]
    mu_ref[:, 2 * D_RWKV:3 * D_RWKV] = _swap_pair_lanes(mu_f[:, 2 * D_RWKV:3 * D_RWKV])
    gng_ref[...] = _swap_pair_lanes(gng_f[...])
    gnb_ref[...] = _swap_pair_lanes(gnb_f[...])


def _tile(j, tile_rows, x_ref, ng_ref, pscale_ref, w0_ref, a0_ref, kk_ref, ka_ref, rk_ref, fg_ref,
          headsum_ref, headswap_ref, cumtri_ref, o_ref, shift_carry, pool_carry, state,
          at_s, rt_s, bt_s, kt_s, bh_s, kh_s, v_s, ge_s,
          win_ref, poolw_ref, mu_ref, lora_ref, gng_ref, gnb_ref, wout_ref):
    @pl.when(j == 0)
    def _():
        shift_carry[...] = jnp.zeros_like(shift_carry)
        pool_carry[...] = jnp.zeros_like(pool_carry)
        state[...] = jnp.zeros_like(state)

    x = x_ref[0, tile_rows, :]
    inv = lax.rsqrt(jnp.mean(x * x, axis=-1, keepdims=True) + NORM_EPS)
    h = (x * inv * ng_ref[...]).astype(BF16)

    row = lax.broadcasted_iota(jnp.int32, (TILE, 1), 0)

    z_sh = jnp.dot(h, win_ref[:, 2 * D_POOL:2 * D_POOL + D_SHIFT], preferred_element_type=F32)

    def shifted(cols):
        zc = z_sh[:, cols]
        prev = pltpu.roll(zc, 1, axis=0)
        head_rows = jnp.where(row[:8] == 0, shift_carry[:, cols], prev[:8])
        prev = jnp.concatenate([head_rows, prev[8:]], axis=0)
        shift_carry[:, cols] = zc[TILE - 1:TILE, :]
        return zc + mu_ref[:, cols] * (prev - zc)

    z_pool = jnp.dot(h, win_ref[:, 0:2 * D_POOL], preferred_element_type=F32)
    pos = (j * TILE + row).astype(F32)
    y_a_parts = []
    for g, w in enumerate(POOL_WINDOWS):
        cols = slice(g * POOL_GROUP, (g + 1) * POOL_GROUP)
        u = z_pool[:, cols]
        s = jnp.concatenate([pool_carry[:, cols], u], axis=0)
        pool_carry[:, cols] = u[TILE - POOL_HALO:, :]
        width = 1
        while width < w:
            s = s + pltpu.roll(s, width, axis=0)
            width *= 2
        inv_cnt = 1.0 / jnp.minimum(pos + 1.0, float(w))
        pooled = s[POOL_HALO:] * inv_cnt - u
        gate_a = _silu(z_pool[:, D_POOL + g * POOL_GROUP:D_POOL + (g + 1) * POOL_GROUP])
        y_a_parts.append(_mm(pooled, poolw_ref[g]) * pscale_ref[:, cols] * gate_a)

    lr = shifted(slice(3 * D_RWKV, D_SHIFT))
    lane = lax.broadcasted_iota(jnp.int32, (1, 2 * LORA), 1)
    lr = jnp.where(lane < LORA, jnp.tanh(lr), lr)
    lora = _mm(lr, lora_ref[...])
    headsum = headsum_ref[...]
    bonus_parts = []
    for p in range(N_HEADS // 2):
        ln = slice(p * PAIR, (p + 1) * PAIR)
        off = lambda base: slice(base + p * PAIR, base + (p + 1) * PAIR)
        r = shifted(ln)
        k = shifted(off(D_RWKV))
        v = shifted(off(2 * D_RWKV))
        ww = w0_ref[:, ln] + lora[:, ln]
        a = _sigmoid(a0_ref[:, ln] + lora[:, off(D_RWKV)])
        logw = -DECAY_SCALE * _sigmoid(ww)

        kkv = k * kk_ref[:, ln]
        ss = _mm(kkv * kkv, headsum)
        kkn = kkv * lax.rsqrt(jnp.maximum(ss, 1e-24))
        k2 = k * (1.0 + (a - 1.0) * ka_ref[:, ln])
        bonus_parts.append(_mm(r * k2 * rk_ref[:, ln], headswap_ref[...]) * v)

        gcum = jnp.concatenate(
            [_mm_exact_lhs(cumtri_ref[...], logw[hb:hb + PAIR], 2) for hb in range(0, TILE, PAIR)],
            axis=0)
        e_inv = jnp.exp(-gcum)
        kka = kkn * a
        at_s[:, ln] = -kkn * jnp.exp(gcum - logw)
        rt_s[:, ln] = r * jnp.exp(gcum)
        bt_s[:, ln] = kka * e_inv
        kt_s[:, ln] = k2 * e_inv
        v_s[:, ln] = v
        for c in range(N_CHUNKS):
            rows = slice(c * CHUNK, (c + 1) * CHUNK)
            gam_c = jnp.exp(gcum[rows.stop - 1:rows.stop, :])
            ge_s[c:c + 1, ln] = gam_c
            e_end = e_inv[rows] * gam_c
            bh_s[rows, ln] = kka[rows] * e_end
            kh_s[rows, ln] = k2[rows] * e_end
    g_b = jnp.dot(h, win_ref[:, 2 * D_POOL + D_SHIFT:], preferred_element_type=F32)
    gate_b = _silu(g_b)

    ri = lax.broadcasted_iota(jnp.int32, (CHUNK, PAIR), 0)
    ci = lax.broadcasted_iota(jnp.int32, (CHUNK, PAIR), 1)
    cm = ci % CHUNK
    left = ci < CHUNK
    strict_k = (ri > cm) & ~left
    strict_b = (ri > cm) & left
    incl = ri >= cm
    eye_r = jnp.where(ci == ri + CHUNK, 1.0, 0.0)
    lane_half = lax.broadcasted_iota(jnp.int32, (1, PAIR), 1) // HEAD
    key_mask = [lane_half == e for e in range(2)]
    eye_key = [ci == ri + e * HEAD for e in range(2)]
    zeros_blk = jnp.zeros((HEAD, PAIR), F32)
    keep_right = jnp.where(lane_half == 1, 1.0, 0.0)
    n_doubling = int(math.log2(CHUNK))
    heads = [(p, e) for p in range(N_HEADS // 2) for e in range(2)]
    chains = [(c, p, e) for c in range(N_CHUNKS) for p, e in heads]
    blk = lambda ref, c, p: ref[c * CHUNK:(c + 1) * CHUNK, p * PAIR:(p + 1) * PAIR]

    aall, vm, z, ry, tx, mgk = [], [], [], [], [], []
    for c, p, e in chains:
        if e == 0:
            lq = jnp.concatenate([blk(at_s, c, p), blk(rt_s, c, p)], axis=0)
            rk_t = jnp.concatenate([blk(bt_s, c, p), blk(kt_s, c, p)], axis=0).T
            rk_pair = jnp.concatenate(
                [jnp.concatenate([rk_t[:HEAD], zeros_blk], axis=1),
                 jnp.concatenate([zeros_blk, rk_t[HEAD:]], axis=1)], axis=0)
            scores = _mm(lq, rk_pair)
        aall.append(scores[:, e * PAIR:(e + 1) * PAIR])
        vm.append(jnp.where(key_mask[e], 0.0, blk(v_s, c, p)))
    for i in range(len(chains)):
        tx.append(jnp.where(strict_b, aall[i][:CHUNK], eye_r))
    per_level = -(-len(chains) // n_doubling)
    for level in range(n_doubling):
        for i in range(len(chains)):
            tx[i] = tx[i] * keep_right + _mm(tx[i][:, :CHUNK], tx[i])
        for i, (c, p, e) in list(enumerate(chains))[level * per_level:(level + 1) * per_level]:
            a_ak = jnp.where(strict_k, aall[i][:CHUNK], 0.0)
            av = _mm(a_ak, jnp.concatenate([vm[i], vm[i]], axis=0))
            z.append(jnp.where(key_mask[e], blk(at_s, c, p), av))
    for i in range(len(chains)):
        z[i] = _mm(tx[i], jnp.concatenate([z[i], z[i]], axis=0))
    for i, (c, p, e) in enumerate(chains):
        a_r = jnp.where(incl, aall[i][CHUNK:], 0.0)
        if e == 0:
            bk_t = jnp.concatenate([blk(bh_s, c, p), blk(kh_s, c, p)], axis=0).T
        both = _mm(jnp.concatenate([a_r, bk_t[e * HEAD:(e + 1) * HEAD]], axis=0),
                   jnp.concatenate([z[i], vm[i]], axis=0))
        ry.append(jnp.where(key_mask[e], blk(rt_s, c, p), 0.0) + both[:CHUNK])
        gam_c = ge_s[c:c + 1, p * PAIR:(p + 1) * PAIR]
        mgk.append(both[CHUNK:] + jnp.where(eye_key[e], gam_c, 0.0))
    y_blocks = []
    for c in range(N_CHUNKS):
        ys = []
        for p in range(N_HEADS // 2):
            i0 = c * N_HEADS + 2 * p
            sp = state[p]
            rhs = jnp.concatenate([jnp.where(key_mask[0], 0.0, sp),
                                   jnp.where(key_mask[0], sp, 0.0)], axis=0)
            ry_pair = jnp.where(key_mask[0], ry[i0], ry[i0 + 1])
            mgk_pair = jnp.where(key_mask[0], mgk[i0], mgk[i0 + 1])
            res = _mm(jnp.concatenate([ry_pair, mgk_pair], axis=0), rhs)
            ys.append(res[:CHUNK] + jnp.where(key_mask[0], ry[i0 + 1], ry[i0]))
            state[p] = res[CHUNK:] + jnp.where(key_mask[0], mgk[i0 + 1], mgk[i0])
        y_blocks.append(jnp.concatenate(ys, axis=1))
    y = jnp.concatenate(y_blocks, axis=0)

    pair_sum = lambda t: jnp.concatenate(
        [_mm(t[:, c:c + PAIR], headsum) for c in range(0, D_RWKV, PAIR)], axis=1)
    mean = pair_sum(y) * (1.0 / HEAD)
    d = y - mean
    var = pair_sum(d * d) * (1.0 / HEAD)
    y = d * lax.rsqrt(var + GN_EPS) * gng_ref[...] + gnb_ref[...]
    y_b = (y + jnp.concatenate(bonus_parts, axis=1)) * gate_b

    yy = jnp.concatenate(y_a_parts + [y_b], axis=1).astype(BF16)
    xo = x + jnp.dot(yy, wout_ref[...], preferred_element_type=F32)
    inv_o = lax.rsqrt(jnp.mean(xo * xo, axis=-1, keepdims=True) + NORM_EPS)
    o_ref[0, tile_rows, :] = xo * inv_o * fg_ref[...]


def _block_kernel(x_ref, ng_ref, win_f, poolw_f, pscale_ref, mu_f, w0_ref, wup_f,
                  a0_ref, aup_f, kk_ref, ka_ref, rk_ref, gng_f, gnb_f, wout_f, fg_ref,
                  headsum_ref, headswap_ref, cumtri_ref,
                  o_ref,
                  shift_carry, pool_carry, state,
                  at_s, rt_s, bt_s, kt_s, bh_s, kh_s, v_s, ge_s,
                  win_ref, poolw_ref, mu_ref, lora_ref, gng_ref, gnb_ref, wout_ref):
    step = pl.program_id(1)

    @pl.when((pl.program_id(0) == 0) & (step == 0))
    def _():
        _prepare_weights(win_f, poolw_f, mu_f, wup_f, aup_f, gng_f, gnb_f, wout_f,
                         win_ref, poolw_ref, mu_ref, lora_ref, gng_ref, gnb_ref, wout_ref)

    def body(i, carry):
        _tile(step * STEP_TILES + i, pl.ds(pl.multiple_of(i * TILE, TILE), TILE),
              x_ref, ng_ref, pscale_ref, w0_ref, a0_ref, kk_ref, ka_ref, rk_ref, fg_ref,
              headsum_ref, headswap_ref, cumtri_ref, o_ref, shift_carry, pool_carry, state,
              at_s, rt_s, bt_s, kt_s, bh_s, kh_s, v_s, ge_s,
              win_ref, poolw_ref, mu_ref, lora_ref, gng_ref, gnb_ref, wout_ref)
        return carry

    lax.fori_loop(0, STEP_TILES, body, 0)


def _full(shape, single_buffer=False):
    index_map = lambda b, j: (0,) * len(shape)
    if single_buffer:
        return pl.BlockSpec(shape, index_map, pipeline_mode=pl.Buffered(1))
    return pl.BlockSpec(shape, index_map)


def kernel(x, norm_gain, w_in, pool_w, pool_scale, shift_mu, w0, w_up, a0, a_up, k_k, k_a, r_k,
           gn_gain, gn_bias, w_out, final_gain):
    batch, seq, d_model = x.shape
    step_rows = STEP_TILES * TILE
    assert norm_gain.shape[0] == 1 and d_model == D_MODEL and seq % step_rows == 0
    assert w_in.shape == (1, D_MODEL, D_IN)

    hid = jnp.arange(PAIR) // HEAD
    headsum = (hid[:, None] == hid[None, :]).astype(BF16)
    headsum_swap = (hid[:, None] != hid[None, :]).astype(BF16)
    t = jnp.arange(PAIR)
    same_chunk = (t[:, None] // CHUNK) == (t[None, :] // CHUNK)
    cumtri = (same_chunk & (t[:, None] >= t[None, :])).astype(BF16)

    operands = (
        x, norm_gain, w_in, pool_w, pool_scale, shift_mu, w0, w_up, a0, a_up, k_k, k_a,
        r_k.reshape(1, D_RWKV), gn_gain, gn_bias, w_out, final_gain.reshape(1, D_MODEL),
        headsum, headsum_swap, cumtri,
    )
    in_specs = [pl.BlockSpec((1, step_rows, D_MODEL), lambda b, j: (b, j, 0))]
    in_specs += [_full(op.shape, single_buffer=op is w_in or op is w_out) for op in operands[1:]]
    seq_buf = lambda: pltpu.VMEM((TILE, D_RWKV), F32)
    return pl.pallas_call(
        _block_kernel,
        grid=(batch, seq // step_rows),
        in_specs=in_specs,
        out_specs=pl.BlockSpec((1, step_rows, D_MODEL), lambda b, j: (b, j, 0)),
        out_shape=jax.ShapeDtypeStruct(x.shape, x.dtype),
        scratch_shapes=[
            pltpu.VMEM((1, D_SHIFT), F32),
            pltpu.VMEM((POOL_HALO, D_POOL), F32),
            pltpu.VMEM((N_HEADS // 2, HEAD, PAIR), F32),
        ] + [seq_buf() for _ in range(7)] + [
            pltpu.VMEM((8, D_RWKV), F32),
            pltpu.VMEM((D_MODEL, D_IN), BF16),
            pltpu.VMEM((len(POOL_WINDOWS), POOL_GROUP, POOL_GROUP), BF16),
            pltpu.VMEM((1, D_SHIFT), F32),
            pltpu.VMEM((2 * LORA, 2 * D_RWKV), BF16),
            pltpu.VMEM((1, D_RWKV), F32),
            pltpu.VMEM((1, D_RWKV), F32),
            pltpu.VMEM((D_MODEL, D_MODEL), BF16),
        ],
        compiler_params=pltpu.CompilerParams(
            dimension_semantics=("arbitrary", "arbitrary"),
            vmem_limit_bytes=VMEM_LIMIT_BYTES),
        name="hybrid_pool_rwkv7_block",
    )(*operands)
```

```python
import math

import jax
import jax.numpy as jnp
from jax import lax
from jax.experimental import pallas as pl
from jax.experimental.pallas import tpu as pltpu

F32 = jnp.float32
BF16 = jnp.bfloat16

D_MODEL = 1024
D_POOL = 512
POOL_WINDOWS = (2, 4, 8, 16)
POOL_GROUP = 128
POOL_HALO = 16
D_RWKV = 512
N_HEADS = 8
HEAD = 64
PAIR = 2 * HEAD
LORA = 64
D_SHIFT = 3 * D_RWKV + 2 * LORA
D_IN = 2 * D_POOL + D_SHIFT + D_RWKV
NORM_EPS = 1e-6
GN_EPS = 64e-5
CHUNK = 64
TILE = 256
STEP_TILES = 4
N_CHUNKS = TILE // CHUNK
DECAY_SCALE = math.exp(-0.5)
VMEM_LIMIT_BYTES = 56 * 1024 * 1024


def _mm(a, b):
    return jnp.dot(a.astype(BF16), b.astype(BF16), preferred_element_type=F32)


def _split(a, parts):
    out = []
    rem = a
    for _ in range(parts):
        p = rem.astype(BF16)
        out.append(p)
        rem = rem - p.astype(F32)
    return out


def _mm_exact_lhs(a_bf16, b, parts):
    acc = None
    for p in _split(b, parts):
        t = jnp.dot(a_bf16, p, preferred_element_type=F32)
        acc = t if acc is None else acc + t
    return acc


def _pair_sum(a, ones_blk):
    return jnp.concatenate([_mm(a[:, c:c + PAIR], ones_blk) for c in range(0, a.shape[1], PAIR)],
                           axis=1)


def _sigmoid(t):
    return 0.5 * jnp.tanh(0.5 * t) + 0.5


def _silu(t):
    return t * _sigmoid(t)


def _swap_pair_lanes(a):
    rows = a.shape[0]
    if rows < 8:
        a = jnp.broadcast_to(a[0:1], (8, a.shape[1]))
    out = jnp.concatenate([pltpu.roll(a[:, c:c + PAIR], HEAD, axis=1)
                           for c in range(0, a.shape[1], PAIR)], axis=1)
    return out[:rows]


def _prepare_weights(win_f, poolw_f, mu_f, wup_f, aup_f, gng_f, gnb_f, wout_f,
                     win_ref, poolw_ref, mu_ref, lora_ref, gng_ref, gnb_ref, wout_ref):
    v0 = 2 * D_POOL + 2 * D_RWKV
    g0 = 2 * D_POOL + D_SHIFT
    for c in range(0, D_IN, PAIR):
        w = win_f[0, :, c:c + PAIR]
        if v0 <= c < v0 + D_RWKV or c >= g0:
            w = pltpu.roll(w, HEAD, axis=1)
        win_ref[:, c:c + PAIR] = w.astype(BF16)
    wout_ref[0:D_POOL, :] = wout_f[0, 0:D_POOL, :].astype(BF16)
    for r in range(D_POOL, D_MODEL, PAIR):
        wout_ref[r:r + HEAD, :] = wout_f[0, r + HEAD:r + PAIR, :].astype(BF16)
        wout_ref[r + HEAD:r + PAIR, :] = wout_f[0, r:r + HEAD, :].astype(BF16)
    poolw_ref[...] = poolw_f[0].astype(BF16)
    lora_ref[...] = jnp.zeros_like(lora_ref)
    lora_ref[0:LORA, 0:D_RWKV] = wup_f[0].astype(BF16)
    lora_ref[LORA:2 * LORA, D_RWKV:2 * D_RWKV] = aup_f[0].astype(BF16)
    mu_ref[...] = mu_f[...]
    mu_ref[:, 2 * D_RWKV:3 * D_RWKV] = _swap_pair_lanes(mu_f[:, 2 * D_RWKV:3 * D_RWKV])
    gng_ref[...] = _swap_pair_lanes(gng_f[...])
    gnb_ref[...] = _swap_pair_lanes(gnb_f[...])


def _tile(j, tile_rows, x_ref, ng_ref, pscale_ref, w0_ref, a0_ref, kk_ref, ka_ref, rk_ref, fg_ref,
          headsum_ref, headswap_ref, cumtri_ref, o_ref, shift_carry, pool_carry, state,
          at_s, rt_s, bt_s, kt_s, bh_s, kh_s, v_s, ge_s,
          win_ref, poolw_ref, mu_ref, lora_ref, gng_ref, gnb_ref, wout_ref):
    @pl.when(j == 0)
    def _():
        shift_carry[...] = jnp.zeros_like(shift_carry)
        pool_carry[...] = jnp.zeros_like(pool_carry)
        state[...] = jnp.zeros_like(state)

    x = x_ref[0, tile_rows, :]
    inv = lax.rsqrt(jnp.mean(x * x, axis=-1, keepdims=True) + NORM_EPS)
    h = (x * inv * ng_ref[...]).astype(BF16)

    row = lax.broadcasted_iota(jnp.int32, (TILE, 1), 0)

    def token_shift(z, cols):
        prev = pltpu.roll(z, 1, axis=0)
        head_rows = jnp.where(row[:8] == 0, shift_carry[:, cols], prev[:8])
        prev = jnp.concatenate([head_rows, prev[8:]], axis=0)
        shift_carry[:, cols] = z[TILE - 1:TILE, :]
        return z + mu_ref[:, cols] * (prev - z)

    lr0 = 2 * D_POOL + 3 * D_RWKV
    z_lr = jnp.dot(h, win_ref[:, lr0:lr0 + 2 * PAIR], preferred_element_type=F32)
    z_rkv = jnp.dot(h, win_ref[:, 2 * D_POOL:lr0], preferred_element_type=F32)
    lr = token_shift(z_lr[:, :PAIR], slice(3 * D_RWKV, D_SHIFT))
    lane = lax.broadcasted_iota(jnp.int32, (1, 2 * LORA), 1)
    lr = jnp.where(lane < LORA, jnp.tanh(lr), lr)
    lora = _mm(lr, lora_ref[...])
    sh = token_shift(z_rkv, slice(0, 3 * D_RWKV))

    z_pool = jnp.dot(h, win_ref[:, 0:2 * D_POOL], preferred_element_type=F32)
    u = z_pool[:, :D_POOL]
    g_a = z_pool[:, D_POOL:]
    ext = jnp.concatenate([pool_carry[...], u], axis=0)
    pool_carry[...] = u[TILE - POOL_HALO:, :]
    pos = (j * TILE + row).astype(F32)
    pooled = []
    s = ext
    for g, w in enumerate(POOL_WINDOWS):
        lo = g * POOL_GROUP
        s = s[:, (POOL_GROUP if g > 0 else 0):]
        s = s + pltpu.roll(s, w // 2, axis=0)
        inv_cnt = 1.0 / jnp.minimum(pos + 1.0, float(w))
        pooled.append(s[POOL_HALO:, :POOL_GROUP] * inv_cnt - u[:, lo:lo + POOL_GROUP])
    mixed = [_mm(pooled[g], poolw_ref[g]) for g in range(len(POOL_WINDOWS))]
    y_a = jnp.concatenate(mixed, axis=1) * pscale_ref[...] * _silu(g_a)

    headsum = headsum_ref[...]
    bonus_parts = []
    for p in range(N_HEADS // 2):
        ln = slice(p * PAIR, (p + 1) * PAIR)
        off = lambda base: slice(base + p * PAIR, base + (p + 1) * PAIR)
        r = sh[:, ln]
        k = sh[:, off(D_RWKV)]
        v = sh[:, off(2 * D_RWKV)]
        ww = w0_ref[:, ln] + lora[:, ln]
        a = _sigmoid(a0_ref[:, ln] + lora[:, off(D_RWKV)])
        logw = -DECAY_SCALE * _sigmoid(ww)

        kkv = k * kk_ref[:, ln]
        ss = _mm(kkv * kkv, headsum)
        kkn = kkv * lax.rsqrt(jnp.maximum(ss, 1e-24))
        k2 = k * (1.0 + (a - 1.0) * ka_ref[:, ln])
        bonus_parts.append(_mm(r * k2 * rk_ref[:, ln], headswap_ref[...]) * v)

        gcum = jnp.concatenate(
            [_mm_exact_lhs(cumtri_ref[...], logw[hb:hb + PAIR], 2) for hb in range(0, TILE, PAIR)],
            axis=0)
        e_inv = jnp.exp(-gcum)
        kka = kkn * a
        at_s[:, ln] = -kkn * jnp.exp(gcum - logw)
        rt_s[:, ln] = r * jnp.exp(gcum)
        bt_s[:, ln] = kka * e_inv
        kt_s[:, ln] = k2 * e_inv
        v_s[:, ln] = v
        for c in range(N_CHUNKS):
            rows = slice(c * CHUNK, (c + 1) * CHUNK)
            gam_c = jnp.exp(gcum[rows.stop - 1:rows.stop, :])
            ge_s[c:c + 1, ln] = gam_c
            e_end = e_inv[rows] * gam_c
            bh_s[rows, ln] = kka[rows] * e_end
            kh_s[rows, ln] = k2[rows] * e_end
    bonus = jnp.concatenate(bonus_parts, axis=1)
    g_rest = jnp.dot(h, win_ref[:, lr0 + 2 * PAIR:], preferred_element_type=F32)
    gate_b = _silu(jnp.concatenate([z_lr[:, PAIR:], g_rest], axis=1))

    ri = lax.broadcasted_iota(jnp.int32, (CHUNK, PAIR), 0)
    ci = lax.broadcasted_iota(jnp.int32, (CHUNK, PAIR), 1)
    cm = ci % CHUNK
    left = ci < CHUNK
    strict_k = (ri > cm) & ~left
    strict_b = (ri > cm) & left
    incl = ri >= cm
    eye_r = jnp.where(ci == ri + CHUNK, 1.0, 0.0)
    lane_half = lax.broadcasted_iota(jnp.int32, (1, PAIR), 1) // HEAD
    key_mask = [lane_half == e for e in range(2)]
    eye_key = [ci == ri + e * HEAD for e in range(2)]
    zeros_blk = jnp.zeros((HEAD, PAIR), F32)
    keep_right = jnp.where(lane_half == 1, 1.0, 0.0)
    n_doubling = int(math.log2(CHUNK))
    heads = [(p, e) for p in range(N_HEADS // 2) for e in range(2)]
    chains = [(c, p, e) for c in range(N_CHUNKS) for p, e in heads]
    blk = lambda ref, c, p: ref[c * CHUNK:(c + 1) * CHUNK, p * PAIR:(p + 1) * PAIR]

    aall, vm, z, ry, tx, mgk = [], [], [], [], [], []
    for c, p, e in chains:
        if e == 0:
            lq = jnp.concatenate([blk(at_s, c, p), blk(rt_s, c, p)], axis=0)
            rk_t = jnp.concatenate([blk(bt_s, c, p), blk(kt_s, c, p)], axis=0).T
            rk_pair = jnp.concatenate(
                [jnp.concatenate([rk_t[:HEAD], zeros_blk], axis=1),
                 jnp.concatenate([zeros_blk, rk_t[HEAD:]], axis=1)], axis=0)
            scores = _mm(lq, rk_pair)
        aall.append(scores[:, e * PAIR:(e + 1) * PAIR])
        vm.append(jnp.where(key_mask[e], 0.0, blk(v_s, c, p)))
    for i in range(len(chains)):
        tx.append(jnp.where(strict_b, aall[i][:CHUNK], eye_r))
    per_level = -(-len(chains) // n_doubling)
    for level in range(n_doubling):
        for i in range(len(chains)):
            tx[i] = tx[i] * keep_right + _mm(tx[i][:, :CHUNK], tx[i])
        for i, (c, p, e) in list(enumerate(chains))[level * per_level:(level + 1) * per_level]:
            a_ak = jnp.where(strict_k, aall[i][:CHUNK], 0.0)
            av = _mm(a_ak, jnp.concatenate([vm[i], vm[i]], axis=0))
            z.append(jnp.where(key_mask[e], blk(at_s, c, p), av))
    for i in range(len(chains)):
        z[i] = _mm(tx[i], jnp.concatenate([z[i], z[i]], axis=0))
    for i, (c, p, e) in enumerate(chains):
        a_r = jnp.where(incl, aall[i][CHUNK:], 0.0)
        if e == 0:
            bk_t = jnp.concatenate([blk(bh_s, c, p), blk(kh_s, c, p)], axis=0).T
        both = _mm(jnp.concatenate([a_r, bk_t[e * HEAD:(e + 1) * HEAD]], axis=0),
                   jnp.concatenate([z[i], vm[i]], axis=0))
        ry.append(jnp.where(key_mask[e], blk(rt_s, c, p), 0.0) + both[:CHUNK])
        gam_c = ge_s[c:c + 1, p * PAIR:(p + 1) * PAIR]
        mgk.append(both[CHUNK:] + jnp.where(eye_key[e], gam_c, 0.0))
    y_blocks = []
    for c in range(N_CHUNKS):
        ys = []
        for p in range(N_HEADS // 2):
            i0 = c * N_HEADS + 2 * p
            sp = state[p]
            rhs = jnp.concatenate([jnp.where(key_mask[0], 0.0, sp),
                                   jnp.where(key_mask[0], sp, 0.0)], axis=0)
            ry_pair = jnp.where(key_mask[0], ry[i0], ry[i0 + 1])
            mgk_pair = jnp.where(key_mask[0], mgk[i0], mgk[i0 + 1])
            res = _mm(jnp.concatenate([ry_pair, mgk_pair], axis=0), rhs)
            ys.append(res[:CHUNK] + jnp.where(key_mask[0], ry[i0 + 1], ry[i0]))
            state[p] = res[CHUNK:] + jnp.where(key_mask[0], mgk[i0 + 1], mgk[i0])
        y_blocks.append(jnp.concatenate(ys, axis=1))
    y = jnp.concatenate(y_blocks, axis=0)

    mean = _pair_sum(y, headsum) * (1.0 / HEAD)
    d = y - mean
    var = _pair_sum(d * d, headsum) * (1.0 / HEAD)
    y = d * lax.rsqrt(var + GN_EPS) * gng_ref[...] + gnb_ref[...]
    y_b = (y + bonus) * gate_b

    yy = jnp.concatenate([y_a, y_b], axis=1).astype(BF16)
    xo = x + jnp.dot(yy, wout_ref[...], preferred_element_type=F32)
    inv_o = lax.rsqrt(jnp.mean(xo * xo, axis=-1, keepdims=True) + NORM_EPS)
    o_ref[0, tile_rows, :] = xo * inv_o * fg_ref[...]


def _block_kernel(x_ref, ng_ref, win_f, poolw_f, pscale_ref, mu_f, w0_ref, wup_f,
                  a0_ref, aup_f, kk_ref, ka_ref, rk_ref, gng_f, gnb_f, wout_f, fg_ref,
                  headsum_ref, headswap_ref, cumtri_ref,
                  o_ref,
                  shift_carry, pool_carry, state,
                  at_s, rt_s, bt_s, kt_s, bh_s, kh_s, v_s, ge_s,
                  win_ref, poolw_ref, mu_ref, lora_ref, gng_ref, gnb_ref, wout_ref):
    step = pl.program_id(1)

    @pl.when((pl.program_id(0) == 0) & (step == 0))
    def _():
        _prepare_weights(win_f, poolw_f, mu_f, wup_f, aup_f, gng_f, gnb_f, wout_f,
                         win_ref, poolw_ref, mu_ref, lora_ref, gng_ref, gnb_ref, wout_ref)

    def body(i, carry):
        _tile(step * STEP_TILES + i, pl.ds(pl.multiple_of(i * TILE, TILE), TILE),
              x_ref, ng_ref, pscale_ref, w0_ref, a0_ref, kk_ref, ka_ref, rk_ref, fg_ref,
              headsum_ref, headswap_ref, cumtri_ref, o_ref, shift_carry, pool_carry, state,
              at_s, rt_s, bt_s, kt_s, bh_s, kh_s, v_s, ge_s,
              win_ref, poolw_ref, mu_ref, lora_ref, gng_ref, gnb_ref, wout_ref)
        return carry

    lax.fori_loop(0, STEP_TILES, body, 0)


def _full(shape, single_buffer=False):
    index_map = lambda b, j: (0,) * len(shape)
    if single_buffer:
        return pl.BlockSpec(shape, index_map, pipeline_mode=pl.Buffered(1))
    return pl.BlockSpec(shape, index_map)


def kernel(x, norm_gain, w_in, pool_w, pool_scale, shift_mu, w0, w_up, a0, a_up, k_k, k_a, r_k,
           gn_gain, gn_bias, w_out, final_gain):
    batch, seq, d_model = x.shape
    step_rows = STEP_TILES * TILE
    assert norm_gain.shape[0] == 1 and d_model == D_MODEL and seq % step_rows == 0
    assert w_in.shape == (1, D_MODEL, D_IN)

    hid = jnp.arange(PAIR) // HEAD
    headsum = (hid[:, None] == hid[None, :]).astype(BF16)
    headsum_swap = (hid[:, None] != hid[None, :]).astype(BF16)
    t = jnp.arange(PAIR)
    same_chunk = (t[:, None] // CHUNK) == (t[None, :] // CHUNK)
    cumtri = (same_chunk & (t[:, None] >= t[None, :])).astype(BF16)

    operands = (
        x, norm_gain, w_in, pool_w, pool_scale, shift_mu, w0, w_up, a0, a_up, k_k, k_a,
        r_k.reshape(1, D_RWKV), gn_gain, gn_bias, w_out, final_gain.reshape(1, D_MODEL),
        headsum, headsum_swap, cumtri,
    )
    in_specs = [pl.BlockSpec((1, step_rows, D_MODEL), lambda b, j: (b, j, 0))]
    in_specs += [_full(op.shape, single_buffer=op is w_in or op is w_out) for op in operands[1:]]
    seq_buf = lambda: pltpu.VMEM((TILE, D_RWKV), F32)
    return pl.pallas_call(
        _block_kernel,
        grid=(batch, seq // step_rows),
        in_specs=in_specs,
        out_specs=pl.BlockSpec((1, step_rows, D_MODEL), lambda b, j: (b, j, 0)),
        out_shape=jax.ShapeDtypeStruct(x.shape, x.dtype),
        scratch_shapes=[
            pltpu.VMEM((1, D_SHIFT), F32),
            pltpu.VMEM((POOL_HALO, D_POOL), F32),
            pltpu.VMEM((N_HEADS // 2, HEAD, PAIR), F32),
        ] + [seq_buf() for _ in range(7)] + [
            pltpu.VMEM((8, D_RWKV), F32),
            pltpu.VMEM((D_MODEL, D_IN), BF16),
            pltpu.VMEM((len(POOL_WINDOWS), POOL_GROUP, POOL_GROUP), BF16),
            pltpu.VMEM((1, D_SHIFT), F32),
            pltpu.VMEM((2 * LORA, 2 * D_RWKV), BF16),
            pltpu.VMEM((1, D_RWKV), F32),
            pltpu.VMEM((1, D_RWKV), F32),
            pltpu.VMEM((D_MODEL, D_MODEL), BF16),
        ],
        compiler_params=pltpu.CompilerParams(
            dimension_semantics=("arbitrary", "arbitrary"),
            vmem_limit_bytes=VMEM_LIMIT_BYTES),
        name="hybrid_pool_rwkv7_block",
    )(*operands)
```

```python
import math

import jax
import jax.numpy as jnp
from jax import lax
from jax.experimental import pallas as pl
from jax.experimental.pallas import tpu as pltpu

F32 = jnp.float32
BF16 = jnp.bfloat16

D_MODEL = 1024
D_POOL = 512
POOL_WINDOWS = (2, 4, 8, 16)
POOL_GROUP = 128
POOL_HALO = 16
D_RWKV = 512
N_HEADS = 8
HEAD = 64
PAIR = 2 * HEAD
LORA = 64
D_SHIFT = 3 * D_RWKV + 2 * LORA
D_IN = 2 * D_POOL + D_SHIFT + D_RWKV
NORM_EPS = 1e-6
GN_EPS = 64e-5
KK_NORM_FLOOR = 1e-12
SUBLANES = 8
CHUNK = 64
TILE = 256
STEP_TILES = 4
N_CHUNKS = TILE // CHUNK
DECAY_SCALE = math.exp(-0.5)
VMEM_LIMIT_BYTES = 56 * 1024 * 1024


def _mm(a, b):
    return jnp.dot(a.astype(BF16), b.astype(BF16), preferred_element_type=F32)


def _split(a, parts):
    out = []
    rem = a
    for _ in range(parts):
        p = rem.astype(BF16)
        out.append(p)
        rem = rem - p.astype(F32)
    return out


def _mm_exact_lhs(a_bf16, b, parts):
    acc = None
    for p in _split(b, parts):
        t = jnp.dot(a_bf16, p, preferred_element_type=F32)
        acc = t if acc is None else acc + t
    return acc


def _pair_sum(a, ones_blk):
    return jnp.concatenate([_mm(a[:, c:c + PAIR], ones_blk) for c in range(0, a.shape[1], PAIR)],
                           axis=1)


def _sigmoid(t):
    return 0.5 * jnp.tanh(0.5 * t) + 0.5


def _silu(t):
    return t * _sigmoid(t)


def _swap_pair_lanes(a):
    rows = a.shape[0]
    if rows < SUBLANES:
        a = jnp.broadcast_to(a[0:1], (SUBLANES, a.shape[1]))
    out = jnp.concatenate([pltpu.roll(a[:, c:c + PAIR], HEAD, axis=1)
                           for c in range(0, a.shape[1], PAIR)], axis=1)
    return out[:rows]


def _prepare_weights(win_f, poolw_f, mu_f, wup_f, aup_f, gng_f, gnb_f, wout_f,
                     win_ref, poolw_ref, mu_ref, lora_ref, gng_ref, gnb_ref, wout_ref):
    v0 = 2 * D_POOL + 2 * D_RWKV
    g0 = 2 * D_POOL + D_SHIFT
    for c in range(0, D_IN, PAIR):
        w = win_f[0, :, c:c + PAIR]
        if v0 <= c < v0 + D_RWKV or c >= g0:
            w = pltpu.roll(w, HEAD, axis=1)
        win_ref[:, c:c + PAIR] = w.astype(BF16)
    wout_ref[0:D_POOL, :] = wout_f[0, 0:D_POOL, :].astype(BF16)
    for r in range(D_POOL, D_MODEL, PAIR):
        wout_ref[r:r + HEAD, :] = wout_f[0, r + HEAD:r + PAIR, :].astype(BF16)
        wout_ref[r + HEAD:r + PAIR, :] = wout_f[0, r:r + HEAD, :].astype(BF16)
    poolw_ref[...] = poolw_f[0].astype(BF16)
    lora_ref[...] = jnp.zeros_like(lora_ref)
    lora_ref[0:LORA, 0:D_RWKV] = wup_f[0].astype(BF16)
    lora_ref[LORA:2 * LORA, D_RWKV:2 * D_RWKV] = aup_f[0].astype(BF16)
    mu_ref[...] = mu_f[...]
    mu_ref[:, 2 * D_RWKV:3 * D_RWKV] = _swap_pair_lanes(mu_f[:, 2 * D_RWKV:3 * D_RWKV])
    gng_ref[...] = _swap_pair_lanes(gng_f[...])
    gnb_ref[...] = _swap_pair_lanes(gnb_f[...])


def _tile(j, tile_rows, x_ref, ng_ref, pscale_ref, w0_ref, a0_ref, kk_ref, ka_ref, rk_ref, fg_ref,
          headsum_ref, headswap_ref, cumtri_ref, o_ref, shift_carry, pool_carry, state,
          at_s, rt_s, bt_s, kt_s, bh_s, kh_s, v_s, ge_s,
          win_ref, poolw_ref, mu_ref, lora_ref, gng_ref, gnb_ref, wout_ref):
    @pl.when(j == 0)
    def _():
        shift_carry[...] = jnp.zeros_like(shift_carry)
        pool_carry[...] = jnp.zeros_like(pool_carry)
        state[...] = jnp.zeros_like(state)

    x = x_ref[0, tile_rows, :]
    inv = lax.rsqrt(jnp.mean(x * x, axis=-1, keepdims=True) + NORM_EPS)
    h = (x * inv * ng_ref[...]).astype(BF16)

    row = lax.broadcasted_iota(jnp.int32, (TILE, 1), 0)

    def token_shift(z, cols):
        prev = pltpu.roll(z, 1, axis=0)
        head_rows = jnp.where(row[:SUBLANES] == 0, shift_carry[:, cols], prev[:SUBLANES])
        prev = jnp.concatenate([head_rows, prev[SUBLANES:]], axis=0)
        shift_carry[:, cols] = z[TILE - 1:TILE, :]
        return z + mu_ref[:, cols] * (prev - z)

    lr0 = 2 * D_POOL + 3 * D_RWKV
    z_lr = jnp.dot(h, win_ref[:, lr0:lr0 + 2 * PAIR], preferred_element_type=F32)
    z_rkv = jnp.dot(h, win_ref[:, 2 * D_POOL:lr0], preferred_element_type=F32)
    lr = token_shift(z_lr[:, :PAIR], slice(3 * D_RWKV, D_SHIFT))
    lane = lax.broadcasted_iota(jnp.int32, (1, 2 * LORA), 1)
    lr = jnp.where(lane < LORA, jnp.tanh(lr), lr)
    lora = _mm(lr, lora_ref[...])
    sh = token_shift(z_rkv, slice(0, 3 * D_RWKV))

    z_pool = jnp.dot(h, win_ref[:, 0:2 * D_POOL], preferred_element_type=F32)
    u = z_pool[:, :D_POOL]
    g_a = z_pool[:, D_POOL:]
    ext = jnp.concatenate([pool_carry[...], u], axis=0)
    pool_carry[...] = u[TILE - POOL_HALO:, :]
    pos = (j * TILE + row).astype(F32)
    pooled = []
    s = ext
    for g, w in enumerate(POOL_WINDOWS):
        lo = g * POOL_GROUP
        s = s[:, (POOL_GROUP if g > 0 else 0):]
        s = s + pltpu.roll(s, w // 2, axis=0)
        inv_cnt = 1.0 / jnp.minimum(pos + 1.0, float(w))
        pooled.append(s[POOL_HALO:, :POOL_GROUP] * inv_cnt - u[:, lo:lo + POOL_GROUP])
    mixed = [_mm(pooled[g], poolw_ref[g]) for g in range(len(POOL_WINDOWS))]
    y_a = jnp.concatenate(mixed, axis=1) * pscale_ref[...] * _silu(g_a)

    headsum = headsum_ref[...]
    bonus_parts = []
    for p in range(N_HEADS // 2):
        ln = slice(p * PAIR, (p + 1) * PAIR)
        off = lambda base: slice(base + p * PAIR, base + (p + 1) * PAIR)
        r = sh[:, ln]
        k = sh[:, off(D_RWKV)]
        v = sh[:, off(2 * D_RWKV)]
        ww = w0_ref[:, ln] + lora[:, ln]
        a = _sigmoid(a0_ref[:, ln] + lora[:, off(D_RWKV)])
        logw = -DECAY_SCALE * _sigmoid(ww)

        kkv = k * kk_ref[:, ln]
        ss = _mm(kkv * kkv, headsum)
        kkn = kkv * lax.rsqrt(jnp.maximum(ss, KK_NORM_FLOOR ** 2))
        k2 = k * (1.0 + (a - 1.0) * ka_ref[:, ln])
        bonus_parts.append(_mm(r * k2 * rk_ref[:, ln], headswap_ref[...]) * v)

        gcum = jnp.concatenate(
            [_mm_exact_lhs(cumtri_ref[...], logw[hb:hb + PAIR], 2) for hb in range(0, TILE, PAIR)],
            axis=0)
        e_inv = jnp.exp(-gcum)
        kka = kkn * a
        at_s[:, ln] = -kkn * jnp.exp(gcum - logw)
        rt_s[:, ln] = r * jnp.exp(gcum)
        bt_s[:, ln] = kka * e_inv
        kt_s[:, ln] = k2 * e_inv
        v_s[:, ln] = v
        for c in range(N_CHUNKS):
            rows = slice(c * CHUNK, (c + 1) * CHUNK)
            gam_c = jnp.exp(gcum[rows.stop - 1:rows.stop, :])
            ge_s[c:c + 1, ln] = gam_c
            e_end = e_inv[rows] * gam_c
            bh_s[rows, ln] = kka[rows] * e_end
            kh_s[rows, ln] = k2[rows] * e_end
    bonus = jnp.concatenate(bonus_parts, axis=1)
    g_rest = jnp.dot(h, win_ref[:, lr0 + 2 * PAIR:], preferred_element_type=F32)
    gate_b = _silu(jnp.concatenate([z_lr[:, PAIR:], g_rest], axis=1))

    ri = lax.broadcasted_iota(jnp.int32, (CHUNK, PAIR), 0)
    ci = lax.broadcasted_iota(jnp.int32, (CHUNK, PAIR), 1)
    cm = ci % CHUNK
    left = ci < CHUNK
    strict_k = (ri > cm) & ~left
    strict_b = (ri > cm) & left
    incl = ri >= cm
    eye_r = jnp.where(ci == ri + CHUNK, 1.0, 0.0)
    lane_half = lax.broadcasted_iota(jnp.int32, (1, PAIR), 1) // HEAD
    key_mask = [lane_half == e for e in range(2)]
    eye_key = [ci == ri + e * HEAD for e in range(2)]
    zeros_blk = jnp.zeros((HEAD, PAIR), F32)
    keep_right = jnp.where(lane_half == 1, 1.0, 0.0)
    n_doubling = int(math.log2(CHUNK))
    heads = [(p, e) for p in range(N_HEADS // 2) for e in range(2)]
    chains = [(c, p, e) for c in range(N_CHUNKS) for p, e in heads]
    blk = lambda ref, c, p: ref[c * CHUNK:(c + 1) * CHUNK, p * PAIR:(p + 1) * PAIR]

    aall, vm, z, ry, tx, mgk = [], [], [], [], [], []
    for c, p, e in chains:
        if e == 0:
            lq = jnp.concatenate([blk(at_s, c, p), blk(rt_s, c, p)], axis=0)
            rk_t = jnp.concatenate([blk(bt_s, c, p), blk(kt_s, c, p)], axis=0).T
            rk_pair = jnp.concatenate(
                [jnp.concatenate([rk_t[:HEAD], zeros_blk], axis=1),
                 jnp.concatenate([zeros_blk, rk_t[HEAD:]], axis=1)], axis=0)
            scores = _mm(lq, rk_pair)
        aall.append(scores[:, e * PAIR:(e + 1) * PAIR])
        vm.append(jnp.where(key_mask[e], 0.0, blk(v_s, c, p)))
    for i in range(len(chains)):
        tx.append(jnp.where(strict_b, aall[i][:CHUNK], eye_r))
    per_level = -(-len(chains) // n_doubling)
    for level in range(n_doubling):
        for i in range(len(chains)):
            tx[i] = tx[i] * keep_right + _mm(tx[i][:, :CHUNK], tx[i])
        for i, (c, p, e) in list(enumerate(chains))[level * per_level:(level + 1) * per_level]:
            a_ak = jnp.where(strict_k, aall[i][:CHUNK], 0.0)
            av = _mm(a_ak, jnp.concatenate([vm[i], vm[i]], axis=0))
            z.append(jnp.where(key_mask[e], blk(at_s, c, p), av))
    for i in range(len(chains)):
        z[i] = _mm(tx[i], jnp.concatenate([z[i], z[i]], axis=0))
    for i, (c, p, e) in enumerate(chains):
        a_r = jnp.where(incl, aall[i][CHUNK:], 0.0)
        if e == 0:
            bk_t = jnp.concatenate([blk(bh_s, c, p), blk(kh_s, c, p)], axis=0).T
        both = _mm(jnp.concatenate([a_r, bk_t[e * HEAD:(e + 1) * HEAD]], axis=0),
                   jnp.concatenate([z[i], vm[i]], axis=0))
        ry.append(jnp.where(key_mask[e], blk(rt_s, c, p), 0.0) + both[:CHUNK])
        gam_c = ge_s[c:c + 1, p * PAIR:(p + 1) * PAIR]
        mgk.append(both[CHUNK:] + jnp.where(eye_key[e], gam_c, 0.0))
    y_blocks = []
    for c in range(N_CHUNKS):
        ys = []
        for p in range(N_HEADS // 2):
            i0 = c * N_HEADS + 2 * p
            sp = state[p]
            rhs = jnp.concatenate([jnp.where(key_mask[0], 0.0, sp),
                                   jnp.where(key_mask[0], sp, 0.0)], axis=0)
            ry_pair = jnp.where(key_mask[0], ry[i0], ry[i0 + 1])
            mgk_pair = jnp.where(key_mask[0], mgk[i0], mgk[i0 + 1])
            res = _mm(jnp.concatenate([ry_pair, mgk_pair], axis=0), rhs)
            ys.append(res[:CHUNK] + jnp.where(key_mask[0], ry[i0 + 1], ry[i0]))
            state[p] = res[CHUNK:] + jnp.where(key_mask[0], mgk[i0 + 1], mgk[i0])
        y_blocks.append(jnp.concatenate(ys, axis=1))
    y = jnp.concatenate(y_blocks, axis=0)

    mean = _pair_sum(y, headsum) * (1.0 / HEAD)
    d = y - mean
    var = _pair_sum(d * d, headsum) * (1.0 / HEAD)
    y = d * lax.rsqrt(var + GN_EPS) * gng_ref[...] + gnb_ref[...]
    y_b = (y + bonus) * gate_b

    yy = jnp.concatenate([y_a, y_b], axis=1).astype(BF16)
    xo = x + jnp.dot(yy, wout_ref[...], preferred_element_type=F32)
    inv_o = lax.rsqrt(jnp.mean(xo * xo, axis=-1, keepdims=True) + NORM_EPS)
    o_ref[0, tile_rows, :] = xo * inv_o * fg_ref[...]


def _block_kernel(x_ref, ng_ref, win_f, poolw_f, pscale_ref, mu_f, w0_ref, wup_f,
                  a0_ref, aup_f, kk_ref, ka_ref, rk_ref, gng_f, gnb_f, wout_f, fg_ref,
                  headsum_ref, headswap_ref, cumtri_ref,
                  o_ref,
                  shift_carry, pool_carry, state,
                  at_s, rt_s, bt_s, kt_s, bh_s, kh_s, v_s, ge_s,
                  win_ref, poolw_ref, mu_ref, lora_ref, gng_ref, gnb_ref, wout_ref):
    step = pl.program_id(1)

    @pl.when((pl.program_id(0) == 0) & (step == 0))
    def _():
        _prepare_weights(win_f, poolw_f, mu_f, wup_f, aup_f, gng_f, gnb_f, wout_f,
                         win_ref, poolw_ref, mu_ref, lora_ref, gng_ref, gnb_ref, wout_ref)

    def body(i, carry):
        _tile(step * STEP_TILES + i, pl.ds(pl.multiple_of(i * TILE, TILE), TILE),
              x_ref, ng_ref, pscale_ref, w0_ref, a0_ref, kk_ref, ka_ref, rk_ref, fg_ref,
              headsum_ref, headswap_ref, cumtri_ref, o_ref, shift_carry, pool_carry, state,
              at_s, rt_s, bt_s, kt_s, bh_s, kh_s, v_s, ge_s,
              win_ref, poolw_ref, mu_ref, lora_ref, gng_ref, gnb_ref, wout_ref)
        return carry

    lax.fori_loop(0, STEP_TILES, body, 0)


def _full(shape, single_buffer=False):
    index_map = lambda b, j: (0,) * len(shape)
    if single_buffer:
        return pl.BlockSpec(shape, index_map, pipeline_mode=pl.Buffered(1))
    return pl.BlockSpec(shape, index_map)


def kernel(x, norm_gain, w_in, pool_w, pool_scale, shift_mu, w0, w_up, a0, a_up, k_k, k_a, r_k,
           gn_gain, gn_bias, w_out, final_gain):
    batch, seq, d_model = x.shape
    step_rows = STEP_TILES * TILE
    assert norm_gain.shape[0] == 1 and d_model == D_MODEL and seq % step_rows == 0
    assert w_in.shape == (1, D_MODEL, D_IN)

    hid = jnp.arange(PAIR) // HEAD
    headsum = (hid[:, None] == hid[None, :]).astype(BF16)
    headsum_swap = (hid[:, None] != hid[None, :]).astype(BF16)
    t = jnp.arange(PAIR)
    same_chunk = (t[:, None] // CHUNK) == (t[None, :] // CHUNK)
    cumtri = (same_chunk & (t[:, None] >= t[None, :])).astype(BF16)

    operands = (
        x, norm_gain, w_in, pool_w, pool_scale, shift_mu, w0, w_up, a0, a_up, k_k, k_a,
        r_k.reshape(1, D_RWKV), gn_gain, gn_bias, w_out, final_gain.reshape(1, D_MODEL),
        headsum, headsum_swap, cumtri,
    )
    in_specs = [pl.BlockSpec((1, step_rows, D_MODEL), lambda b, j: (b, j, 0))]
    in_specs += [_full(op.shape, single_buffer=op is w_in or op is w_out) for op in operands[1:]]
    seq_buf = lambda: pltpu.VMEM((TILE, D_RWKV), F32)
    return pl.pallas_call(
        _block_kernel,
        grid=(batch, seq // step_rows),
        in_specs=in_specs,
        out_specs=pl.BlockSpec((1, step_rows, D_MODEL), lambda b, j: (b, j, 0)),
        out_shape=jax.ShapeDtypeStruct(x.shape, x.dtype),
        scratch_shapes=[
            pltpu.VMEM((1, D_SHIFT), F32),
            pltpu.VMEM((POOL_HALO, D_POOL), F32),
            pltpu.VMEM((N_HEADS // 2, HEAD, PAIR), F32),
        ] + [seq_buf() for _ in range(7)] + [
            pltpu.VMEM((max(N_CHUNKS, SUBLANES), D_RWKV), F32),
            pltpu.VMEM((D_MODEL, D_IN), BF16),
            pltpu.VMEM((len(POOL_WINDOWS), POOL_GROUP, POOL_GROUP), BF16),
            pltpu.VMEM((1, D_SHIFT), F32),
            pltpu.VMEM((2 * LORA, 2 * D_RWKV), BF16),
            pltpu.VMEM((1, D_RWKV), F32),
            pltpu.VMEM((1, D_RWKV), F32),
            pltpu.VMEM((D_MODEL, D_MODEL), BF16),
        ],
        compiler_params=pltpu.CompilerParams(
            dimension_semantics=("arbitrary", "arbitrary"),
            vmem_limit_bytes=VMEM_LIMIT_BYTES),
        name="hybrid_pool_rwkv7_block",
    )(*operands)
```

```python
import math

import jax
import jax.numpy as jnp
from jax import lax
from jax.experimental import pallas as pl
from jax.experimental.pallas import tpu as pltpu

F32 = jnp.float32
BF16 = jnp.bfloat16

D_MODEL = 1024
D_POOL = 512
POOL_WINDOWS = (2, 4, 8, 16)
POOL_GROUP = 128
POOL_HALO = 16
D_RWKV = 512
N_HEADS = 8
HEAD = 64
PAIR = 2 * HEAD
LORA = 64
D_SHIFT = 3 * D_RWKV + 2 * LORA
D_IN = 2 * D_POOL + D_SHIFT + D_RWKV
NORM_EPS = 1e-6
GN_EPS = 64e-5
KK_NORM_FLOOR = 1e-12
SUBLANES = 8
CHUNK = 64
TILE = 256
DUO = 2
ROWS = DUO * TILE
STEP_TILES = 1
N_CHUNKS = TILE // CHUNK
DECAY_SCALE = math.exp(-0.5)
VMEM_LIMIT_BYTES = 56 * 1024 * 1024


def _mm(a, b):
    return jnp.dot(a.astype(BF16), b.astype(BF16), preferred_element_type=F32)


def _split(a, parts):
    out = []
    rem = a
    for _ in range(parts):
        p = rem.astype(BF16)
        out.append(p)
        rem = rem - p.astype(F32)
    return out


def _mm_exact_lhs(a_bf16, b, parts):
    acc = None
    for p in _split(b, parts):
        t = jnp.dot(a_bf16, p, preferred_element_type=F32)
        acc = t if acc is None else acc + t
    return acc


def _pair_sum(a, ones_blk):
    return jnp.concatenate([_mm(a[:, c:c + PAIR], ones_blk) for c in range(0, a.shape[1], PAIR)],
                           axis=1)


def _sigmoid(t):
    return 0.5 * jnp.tanh(0.5 * t) + 0.5


def _silu(t):
    return t * _sigmoid(t)


def _swap_pair_lanes(a):
    rows = a.shape[0]
    if rows < SUBLANES:
        a = jnp.broadcast_to(a[0:1], (SUBLANES, a.shape[1]))
    out = jnp.concatenate([pltpu.roll(a[:, c:c + PAIR], HEAD, axis=1)
                           for c in range(0, a.shape[1], PAIR)], axis=1)
    return out[:rows]


def _prepare_weights(win_f, poolw_f, mu_f, wup_f, aup_f, gng_f, gnb_f, wout_f,
                     win_ref, poolw_ref, mu_ref, lora_ref, gng_ref, gnb_ref, wout_ref):
    v0 = 2 * D_POOL + 2 * D_RWKV
    g0 = 2 * D_POOL + D_SHIFT
    for c in range(0, D_IN, PAIR):
        w = win_f[0, :, c:c + PAIR]
        if v0 <= c < v0 + D_RWKV or c >= g0:
            w = pltpu.roll(w, HEAD, axis=1)
        win_ref[:, c:c + PAIR] = w.astype(BF16)
    wout_ref[0:D_POOL, :] = wout_f[0, 0:D_POOL, :].astype(BF16)
    for r in range(D_POOL, D_MODEL, PAIR):
        wout_ref[r:r + HEAD, :] = wout_f[0, r + HEAD:r + PAIR, :].astype(BF16)
        wout_ref[r + HEAD:r + PAIR, :] = wout_f[0, r:r + HEAD, :].astype(BF16)
    poolw_ref[...] = poolw_f[0].astype(BF16)
    lora_ref[...] = jnp.zeros_like(lora_ref)
    lora_ref[0:LORA, 0:D_RWKV] = wup_f[0].astype(BF16)
    lora_ref[LORA:2 * LORA, D_RWKV:2 * D_RWKV] = aup_f[0].astype(BF16)
    mu_ref[...] = mu_f[...]
    mu_ref[:, 2 * D_RWKV:3 * D_RWKV] = _swap_pair_lanes(mu_f[:, 2 * D_RWKV:3 * D_RWKV])
    gng_ref[...] = _swap_pair_lanes(gng_f[...])
    gnb_ref[...] = _swap_pair_lanes(gnb_f[...])


def _tile(j, tile_rows, x_ref, ng_ref, pscale_ref, w0_ref, a0_ref, kk_ref, ka_ref, rk_ref, fg_ref,
          headsum_ref, headswap_ref, cumtri_ref, o_ref, shift_carry, pool_carry, state,
          at_s, rt_s, bt_s, kt_s, bh_s, kh_s, v_s, ge_s,
          win_ref, poolw_ref, mu_ref, lora_ref, gng_ref, gnb_ref, wout_ref):
    @pl.when(j == 0)
    def _():
        shift_carry[...] = jnp.zeros_like(shift_carry)
        pool_carry[...] = jnp.zeros_like(pool_carry)
        state[...] = jnp.zeros_like(state)

    x = jnp.concatenate([x_ref[s, tile_rows, :] for s in range(DUO)], axis=0)
    inv = lax.rsqrt(jnp.mean(x * x, axis=-1, keepdims=True) + NORM_EPS)
    h = (x * inv * ng_ref[...]).astype(BF16)

    row = lax.broadcasted_iota(jnp.int32, (ROWS, 1), 0) % TILE

    def token_shift(z, cols):
        prev = pltpu.roll(z, 1, axis=0)
        parts = []
        for s in range(DUO):
            r0 = s * TILE
            parts.append(jnp.where(row[:SUBLANES] == 0, shift_carry[s:s + 1, cols],
                                   prev[r0:r0 + SUBLANES]))
            parts.append(prev[r0 + SUBLANES:r0 + TILE])
            shift_carry[s:s + 1, cols] = z[r0 + TILE - 1:r0 + TILE, :]
        return z + mu_ref[:, cols] * (jnp.concatenate(parts, axis=0) - z)

    lr0 = 2 * D_POOL + 3 * D_RWKV
    z_lr = jnp.dot(h, win_ref[:, lr0:lr0 + 2 * PAIR], preferred_element_type=F32)
    z_rkv = jnp.dot(h, win_ref[:, 2 * D_POOL:lr0], preferred_element_type=F32)
    lr = token_shift(z_lr[:, :PAIR], slice(3 * D_RWKV, D_SHIFT))
    lane = lax.broadcasted_iota(jnp.int32, (1, 2 * LORA), 1)
    lr = jnp.where(lane < LORA, jnp.tanh(lr), lr)
    lora = _mm(lr, lora_ref[...])
    sh = token_shift(z_rkv, slice(0, 3 * D_RWKV))

    z_pool = jnp.dot(h, win_ref[:, 0:2 * D_POOL], preferred_element_type=F32)
    u = z_pool[:, :D_POOL]
    g_a = z_pool[:, D_POOL:]
    ext = jnp.concatenate(
        [blk for s in range(DUO) for blk in (pool_carry[s], u[s * TILE:(s + 1) * TILE])], axis=0)
    for s in range(DUO):
        pool_carry[s] = u[(s + 1) * TILE - POOL_HALO:(s + 1) * TILE, :]
    seq_rows = lambda t: jnp.concatenate(
        [t[s * (POOL_HALO + TILE) + POOL_HALO:(s + 1) * (POOL_HALO + TILE)] for s in range(DUO)],
        axis=0)
    pos = (j * TILE + row).astype(F32)
    pooled = []
    s = ext
    for g, w in enumerate(POOL_WINDOWS):
        lo = g * POOL_GROUP
        s = s[:, (POOL_GROUP if g > 0 else 0):]
        s = s + pltpu.roll(s, w // 2, axis=0)
        inv_cnt = 1.0 / jnp.minimum(pos + 1.0, float(w))
        pooled.append(seq_rows(s[:, :POOL_GROUP]) * inv_cnt - u[:, lo:lo + POOL_GROUP])
    mixed = [_mm(pooled[g], poolw_ref[g]) for g in range(len(POOL_WINDOWS))]
    y_a = jnp.concatenate(mixed, axis=1) * pscale_ref[...] * _silu(g_a)

    headsum = headsum_ref[...]
    bonus_parts = []
    for p in range(N_HEADS // 2):
        ln = slice(p * PAIR, (p + 1) * PAIR)
        off = lambda base: slice(base + p * PAIR, base + (p + 1) * PAIR)
        r = sh[:, ln]
        k = sh[:, off(D_RWKV)]
        v = sh[:, off(2 * D_RWKV)]
        ww = w0_ref[:, ln] + lora[:, ln]
        a = _sigmoid(a0_ref[:, ln] + lora[:, off(D_RWKV)])
        logw = -DECAY_SCALE * _sigmoid(ww)

        kkv = k * kk_ref[:, ln]
        ss = _mm(kkv * kkv, headsum)
        kkn = kkv * lax.rsqrt(jnp.maximum(ss, KK_NORM_FLOOR ** 2))
        k2 = k * (1.0 + (a - 1.0) * ka_ref[:, ln])
        bonus_parts.append(_mm(r * k2 * rk_ref[:, ln], headswap_ref[...]) * v)

        gcum = jnp.concatenate(
            [_mm_exact_lhs(cumtri_ref[...], logw[hb:hb + PAIR], 2) for hb in range(0, ROWS, PAIR)],
            axis=0)
        e_inv = jnp.exp(-gcum)
        kka = kkn * a
        at_s[:, ln] = -kkn * jnp.exp(gcum - logw)
        rt_s[:, ln] = r * jnp.exp(gcum)
        bt_s[:, ln] = kka * e_inv
        kt_s[:, ln] = k2 * e_inv
        v_s[:, ln] = v
        for c in range(DUO * N_CHUNKS):
            rows = slice(c * CHUNK, (c + 1) * CHUNK)
            gam_c = jnp.exp(gcum[rows.stop - 1:rows.stop, :])
            ge_s[c:c + 1, ln] = gam_c
            e_end = e_inv[rows] * gam_c
            bh_s[rows, ln] = kka[rows] * e_end
            kh_s[rows, ln] = k2[rows] * e_end
    bonus = jnp.concatenate(bonus_parts, axis=1)
    g_rest = jnp.dot(h, win_ref[:, lr0 + 2 * PAIR:], preferred_element_type=F32)
    gate_b = _silu(jnp.concatenate([z_lr[:, PAIR:], g_rest], axis=1))

    ri = lax.broadcasted_iota(jnp.int32, (CHUNK, PAIR), 0)
    ci = lax.broadcasted_iota(jnp.int32, (CHUNK, PAIR), 1)
    cm = ci % CHUNK
    left = ci < CHUNK
    strict_k = (ri > cm) & ~left
    strict_b = (ri > cm) & left
    incl = ri >= cm
    eye_r = jnp.where(ci == ri + CHUNK, 1.0, 0.0)
    lane_half = lax.broadcasted_iota(jnp.int32, (1, PAIR), 1) // HEAD
    key_mask = [lane_half == e for e in range(2)]
    eye_key = [ci == ri + e * HEAD for e in range(2)]
    zeros_blk = jnp.zeros((HEAD, PAIR), F32)
    keep_right = jnp.where(lane_half == 1, 1.0, 0.0)
    n_doubling = int(math.log2(CHUNK))
    heads = [(p, e) for p in range(N_HEADS // 2) for e in range(2)]
    chains = [(c, p, e) for c in range(DUO * N_CHUNKS) for p, e in heads]
    blk = lambda ref, c, p: ref[c * CHUNK:(c + 1) * CHUNK, p * PAIR:(p + 1) * PAIR]

    aall, vm, z, ry, tx, mgk = [], [], [], [], [], []
    for c, p, e in chains:
        if e == 0:
            lq = jnp.concatenate([blk(at_s, c, p), blk(rt_s, c, p)], axis=0)
            rk_t = jnp.concatenate([blk(bt_s, c, p), blk(kt_s, c, p)], axis=0).T
            rk_pair = jnp.concatenate(
                [jnp.concatenate([rk_t[:HEAD], zeros_blk], axis=1),
                 jnp.concatenate([zeros_blk, rk_t[HEAD:]], axis=1)], axis=0)
            scores = _mm(lq, rk_pair)
        aall.append(scores[:, e * PAIR:(e + 1) * PAIR])
        vm.append(jnp.where(key_mask[e], 0.0, blk(v_s, c, p)))
    for i in range(len(chains)):
        tx.append(jnp.where(strict_b, aall[i][:CHUNK], eye_r))
    per_level = -(-len(chains) // n_doubling)
    for level in range(n_doubling):
        for i in range(len(chains)):
            tx[i] = tx[i] * keep_right + _mm(tx[i][:, :CHUNK], tx[i])
        for i, (c, p, e) in list(enumerate(chains))[level * per_level:(level + 1) * per_level]:
            a_ak = jnp.where(strict_k, aall[i][:CHUNK], 0.0)
            av = _mm(a_ak, jnp.concatenate([vm[i], vm[i]], axis=0))
            z.append(jnp.where(key_mask[e], blk(at_s, c, p), av))
    for i in range(len(chains)):
        z[i] = _mm(tx[i], jnp.concatenate([z[i], z[i]], axis=0))
    for i, (c, p, e) in enumerate(chains):
        a_r = jnp.where(incl, aall[i][CHUNK:], 0.0)
        if e == 0:
            bk_t = jnp.concatenate([blk(bh_s, c, p), blk(kh_s, c, p)], axis=0).T
        both = _mm(jnp.concatenate([a_r, bk_t[e * HEAD:(e + 1) * HEAD]], axis=0),
                   jnp.concatenate([z[i], vm[i]], axis=0))
        ry.append(jnp.where(key_mask[e], blk(rt_s, c, p), 0.0) + both[:CHUNK])
        gam_c = ge_s[c:c + 1, p * PAIR:(p + 1) * PAIR]
        mgk.append(both[CHUNK:] + jnp.where(eye_key[e], gam_c, 0.0))
    y_blocks = [None] * (DUO * N_CHUNKS)
    for c, s in [(c, s) for c in range(N_CHUNKS) for s in range(DUO)]:
        ys = []
        for p in range(N_HEADS // 2):
            i0 = (s * N_CHUNKS + c) * N_HEADS + 2 * p
            sp = state[s, p]
            rhs = jnp.concatenate([jnp.where(key_mask[0], 0.0, sp),
                                   jnp.where(key_mask[0], sp, 0.0)], axis=0)
            ry_pair = jnp.where(key_mask[0], ry[i0], ry[i0 + 1])
            mgk_pair = jnp.where(key_mask[0], mgk[i0], mgk[i0 + 1])
            res = _mm(jnp.concatenate([ry_pair, mgk_pair], axis=0), rhs)
            ys.append(res[:CHUNK] + jnp.where(key_mask[0], ry[i0 + 1], ry[i0]))
            state[s, p] = res[CHUNK:] + jnp.where(key_mask[0], mgk[i0 + 1], mgk[i0])
        y_blocks[s * N_CHUNKS + c] = jnp.concatenate(ys, axis=1)
    y = jnp.concatenate(y_blocks, axis=0)

    mean = _pair_sum(y, headsum) * (1.0 / HEAD)
    d = y - mean
    var = _pair_sum(d * d, headsum) * (1.0 / HEAD)
    y = d * lax.rsqrt(var + GN_EPS) * gng_ref[...] + gnb_ref[...]
    y_b = (y + bonus) * gate_b

    yy = jnp.concatenate([y_a, y_b], axis=1).astype(BF16)
    xo = x + jnp.dot(yy, wout_ref[...], preferred_element_type=F32)
    inv_o = lax.rsqrt(jnp.mean(xo * xo, axis=-1, keepdims=True) + NORM_EPS)
    out = xo * inv_o * fg_ref[...]
    for s in range(DUO):
        o_ref[s, tile_rows, :] = out[s * TILE:(s + 1) * TILE]


def _block_kernel(x_ref, ng_ref, win_f, poolw_f, pscale_ref, mu_f, w0_ref, wup_f,
                  a0_ref, aup_f, kk_ref, ka_ref, rk_ref, gng_f, gnb_f, wout_f, fg_ref,
                  headsum_ref, headswap_ref, cumtri_ref,
                  o_ref,
                  shift_carry, pool_carry, state,
                  at_s, rt_s, bt_s, kt_s, bh_s, kh_s, v_s, ge_s,
                  win_ref, poolw_ref, mu_ref, lora_ref, gng_ref, gnb_ref, wout_ref):
    step = pl.program_id(1)

    @pl.when((pl.program_id(0) == 0) & (step == 0))
    def _():
        _prepare_weights(win_f, poolw_f, mu_f, wup_f, aup_f, gng_f, gnb_f, wout_f,
                         win_ref, poolw_ref, mu_ref, lora_ref, gng_ref, gnb_ref, wout_ref)

    def body(i, carry):
        _tile(step * STEP_TILES + i, pl.ds(pl.multiple_of(i * TILE, TILE), TILE),
              x_ref, ng_ref, pscale_ref, w0_ref, a0_ref, kk_ref, ka_ref, rk_ref, fg_ref,
              headsum_ref, headswap_ref, cumtri_ref, o_ref, shift_carry, pool_carry, state,
              at_s, rt_s, bt_s, kt_s, bh_s, kh_s, v_s, ge_s,
              win_ref, poolw_ref, mu_ref, lora_ref, gng_ref, gnb_ref, wout_ref)
        return carry

    lax.fori_loop(0, STEP_TILES, body, 0)


def _full(shape, single_buffer=False):
    index_map = lambda b, j: (0,) * len(shape)
    if single_buffer:
        return pl.BlockSpec(shape, index_map, pipeline_mode=pl.Buffered(1))
    return pl.BlockSpec(shape, index_map)


def kernel(x, norm_gain, w_in, pool_w, pool_scale, shift_mu, w0, w_up, a0, a_up, k_k, k_a, r_k,
           gn_gain, gn_bias, w_out, final_gain):
    batch, seq, d_model = x.shape
    step_rows = STEP_TILES * TILE
    assert norm_gain.shape[0] == 1 and d_model == D_MODEL and seq % step_rows == 0
    assert batch % DUO == 0
    assert w_in.shape == (1, D_MODEL, D_IN)

    hid = jnp.arange(PAIR) // HEAD
    headsum = (hid[:, None] == hid[None, :]).astype(BF16)
    headsum_swap = (hid[:, None] != hid[None, :]).astype(BF16)
    t = jnp.arange(PAIR)
    same_chunk = (t[:, None] // CHUNK) == (t[None, :] // CHUNK)
    cumtri = (same_chunk & (t[:, None] >= t[None, :])).astype(BF16)

    operands = (
        x, norm_gain, w_in, pool_w, pool_scale, shift_mu, w0, w_up, a0, a_up, k_k, k_a,
        r_k.reshape(1, D_RWKV), gn_gain, gn_bias, w_out, final_gain.reshape(1, D_MODEL),
        headsum, headsum_swap, cumtri,
    )
    in_specs = [pl.BlockSpec((DUO, step_rows, D_MODEL), lambda b, j: (b, j, 0))]
    in_specs += [_full(op.shape, single_buffer=op is w_in or op is w_out) for op in operands[1:]]
    seq_buf = lambda: pltpu.VMEM((ROWS, D_RWKV), F32)
    return pl.pallas_call(
        _block_kernel,
        grid=(batch // DUO, seq // step_rows),
        in_specs=in_specs,
        out_specs=pl.BlockSpec((DUO, step_rows, D_MODEL), lambda b, j: (b, j, 0)),
        out_shape=jax.ShapeDtypeStruct(x.shape, x.dtype),
        scratch_shapes=[
            pltpu.VMEM((DUO, D_SHIFT), F32),
            pltpu.VMEM((DUO, POOL_HALO, D_POOL), F32),
            pltpu.VMEM((DUO, N_HEADS // 2, HEAD, PAIR), F32),
        ] + [seq_buf() for _ in range(7)] + [
            pltpu.VMEM((max(DUO * N_CHUNKS, SUBLANES), D_RWKV), F32),
            pltpu.VMEM((D_MODEL, D_IN), BF16),
            pltpu.VMEM((len(POOL_WINDOWS), POOL_GROUP, POOL_GROUP), BF16),
            pltpu.VMEM((1, D_SHIFT), F32),
            pltpu.VMEM((2 * LORA, 2 * D_RWKV), BF16),
            pltpu.VMEM((1, D_RWKV), F32),
            pltpu.VMEM((1, D_RWKV), F32),
            pltpu.VMEM((D_MODEL, D_MODEL), BF16),
        ],
        compiler_params=pltpu.CompilerParams(
            dimension_semantics=("arbitrary", "arbitrary"),
            vmem_limit_bytes=VMEM_LIMIT_BYTES),
        name="hybrid_pool_rwkv7_block",
    )(*operands)
```

```python
import math

import jax
import jax.numpy as jnp
from jax import lax
from jax.experimental import pallas as pl
from jax.experimental.pallas import tpu as pltpu

F32 = jnp.float32
BF16 = jnp.bfloat16

D_MODEL = 1024
D_POOL = 512
POOL_WINDOWS = (2, 4, 8, 16)
POOL_GROUP = 128
POOL_HALO = 16
D_RWKV = 512
N_HEADS = 8
HEAD = 64
PAIR = 2 * HEAD
LORA = 64
D_SHIFT = 3 * D_RWKV + 2 * LORA
D_IN = 2 * D_POOL + D_SHIFT + D_RWKV
NORM_EPS = 1e-6
GN_EPS = 64e-5
KK_NORM_FLOOR = 1e-12
SUBLANES = 8
CHUNK = 64
TILE = 256
DUO = 2
ROWS = DUO * TILE
STEP_TILES = 2
N_CHUNKS = TILE // CHUNK
DECAY_SCALE = math.exp(-0.5)
VMEM_LIMIT_BYTES = 60 * 1024 * 1024


def _mm(a, b):
    return jnp.dot(a.astype(BF16), b.astype(BF16), preferred_element_type=F32)


def _split(a, parts):
    out = []
    rem = a
    for _ in range(parts):
        p = rem.astype(BF16)
        out.append(p)
        rem = rem - p.astype(F32)
    return out


def _mm_exact_lhs(a_bf16, b, parts):
    acc = None
    for p in _split(b, parts):
        t = jnp.dot(a_bf16, p, preferred_element_type=F32)
        acc = t if acc is None else acc + t
    return acc


def _pair_sum(a, ones_blk):
    return jnp.concatenate([_mm(a[:, c:c + PAIR], ones_blk) for c in range(0, a.shape[1], PAIR)],
                           axis=1)


def _sigmoid(t):
    return 0.5 * jnp.tanh(0.5 * t) + 0.5


def _silu(t):
    return t * _sigmoid(t)


def _swap_pair_lanes(a):
    rows = a.shape[0]
    if rows < SUBLANES:
        a = jnp.broadcast_to(a[0:1], (SUBLANES, a.shape[1]))
    out = jnp.concatenate([pltpu.roll(a[:, c:c + PAIR], HEAD, axis=1)
                           for c in range(0, a.shape[1], PAIR)], axis=1)
    return out[:rows]


def _prepare_weights(win_f, poolw_f, mu_f, wup_f, aup_f, gng_f, gnb_f, wout_f,
                     win_ref, poolw_ref, mu_ref, lora_ref, gng_ref, gnb_ref, wout_ref):
    v0 = 2 * D_POOL + 2 * D_RWKV
    g0 = 2 * D_POOL + D_SHIFT
    for c in range(0, D_IN, PAIR):
        w = win_f[0, :, c:c + PAIR]
        if v0 <= c < v0 + D_RWKV or c >= g0:
            w = pltpu.roll(w, HEAD, axis=1)
        win_ref[:, c:c + PAIR] = w.astype(BF16)
    wout_ref[0:D_POOL, :] = wout_f[0, 0:D_POOL, :].astype(BF16)
    for r in range(D_POOL, D_MODEL, PAIR):
        wout_ref[r:r + HEAD, :] = wout_f[0, r + HEAD:r + PAIR, :].astype(BF16)
        wout_ref[r + HEAD:r + PAIR, :] = wout_f[0, r:r + HEAD, :].astype(BF16)
    poolw_ref[...] = poolw_f[0].astype(BF16)
    lora_ref[...] = jnp.zeros_like(lora_ref)
    lora_ref[0:LORA, 0:D_RWKV] = wup_f[0].astype(BF16)
    lora_ref[LORA:2 * LORA, D_RWKV:2 * D_RWKV] = aup_f[0].astype(BF16)
    mu_ref[...] = mu_f[...]
    mu_ref[:, 2 * D_RWKV:3 * D_RWKV] = _swap_pair_lanes(mu_f[:, 2 * D_RWKV:3 * D_RWKV])
    gng_ref[...] = _swap_pair_lanes(gng_f[...])
    gnb_ref[...] = _swap_pair_lanes(gnb_f[...])


def _tile(j, tile_rows, x_ref, ng_ref, pscale_ref, w0_ref, a0_ref, kk_ref, ka_ref, rk_ref, fg_ref,
          headsum_ref, headswap_ref, cumtri_ref, o_ref, shift_carry, pool_carry, state,
          at_s, rt_s, bt_s, kt_s, bh_s, kh_s, v_s, ge_s,
          win_ref, poolw_ref, mu_ref, lora_ref, gng_ref, gnb_ref, wout_ref):
    @pl.when(j == 0)
    def _():
        shift_carry[...] = jnp.zeros_like(shift_carry)
        pool_carry[...] = jnp.zeros_like(pool_carry)
        state[...] = jnp.zeros_like(state)

    x = jnp.concatenate([x_ref[s, tile_rows, :] for s in range(DUO)], axis=0)
    inv = lax.rsqrt(jnp.mean(x * x, axis=-1, keepdims=True) + NORM_EPS)
    h = (x * inv * ng_ref[...]).astype(BF16)

    row = lax.broadcasted_iota(jnp.int32, (ROWS, 1), 0) % TILE

    def token_shift(z, cols):
        prev = pltpu.roll(z, 1, axis=0)
        parts = []
        for s in range(DUO):
            r0 = s * TILE
            parts.append(jnp.where(row[:SUBLANES] == 0, shift_carry[s:s + 1, cols],
                                   prev[r0:r0 + SUBLANES]))
            parts.append(prev[r0 + SUBLANES:r0 + TILE])
            shift_carry[s:s + 1, cols] = z[r0 + TILE - 1:r0 + TILE, :]
        return z + mu_ref[:, cols] * (jnp.concatenate(parts, axis=0) - z)

    lr0 = 2 * D_POOL + 3 * D_RWKV
    z_lr = jnp.dot(h, win_ref[:, lr0:lr0 + 2 * PAIR], preferred_element_type=F32)
    z_rkv = jnp.dot(h, win_ref[:, 2 * D_POOL:lr0], preferred_element_type=F32)
    lr = token_shift(z_lr[:, :PAIR], slice(3 * D_RWKV, D_SHIFT))
    lane = lax.broadcasted_iota(jnp.int32, (1, 2 * LORA), 1)
    lr = jnp.where(lane < LORA, jnp.tanh(lr), lr)
    lora = _mm(lr, lora_ref[...])
    sh = token_shift(z_rkv, slice(0, 3 * D_RWKV))

    z_pool = jnp.dot(h, win_ref[:, 0:2 * D_POOL], preferred_element_type=F32)
    u = z_pool[:, :D_POOL]
    g_a = z_pool[:, D_POOL:]
    ext = jnp.concatenate(
        [blk for s in range(DUO) for blk in (pool_carry[s], u[s * TILE:(s + 1) * TILE])], axis=0)
    for s in range(DUO):
        pool_carry[s] = u[(s + 1) * TILE - POOL_HALO:(s + 1) * TILE, :]
    seq_rows = lambda t: jnp.concatenate(
        [t[s * (POOL_HALO + TILE) + POOL_HALO:(s + 1) * (POOL_HALO + TILE)] for s in range(DUO)],
        axis=0)
    pos = (j * TILE + row).astype(F32)
    pooled = []
    s = ext
    for g, w in enumerate(POOL_WINDOWS):
        lo = g * POOL_GROUP
        s = s[:, (POOL_GROUP if g > 0 else 0):]
        s = s + pltpu.roll(s, w // 2, axis=0)
        inv_cnt = 1.0 / jnp.minimum(pos + 1.0, float(w))
        pooled.append(seq_rows(s[:, :POOL_GROUP]) * inv_cnt - u[:, lo:lo + POOL_GROUP])
    mixed = [_mm(pooled[g], poolw_ref[g]) for g in range(len(POOL_WINDOWS))]
    y_a = jnp.concatenate(mixed, axis=1) * pscale_ref[...] * _silu(g_a)

    headsum = headsum_ref[...]
    bonus_parts = []
    for p in range(N_HEADS // 2):
        ln = slice(p * PAIR, (p + 1) * PAIR)
        off = lambda base: slice(base + p * PAIR, base + (p + 1) * PAIR)
        r = sh[:, ln]
        k = sh[:, off(D_RWKV)]
        v = sh[:, off(2 * D_RWKV)]
        ww = w0_ref[:, ln] + lora[:, ln]
        a = _sigmoid(a0_ref[:, ln] + lora[:, off(D_RWKV)])
        logw = -DECAY_SCALE * _sigmoid(ww)

        kkv = k * kk_ref[:, ln]
        ss = _mm(kkv * kkv, headsum)
        kkn = kkv * lax.rsqrt(jnp.maximum(ss, KK_NORM_FLOOR ** 2))
        k2 = k * (1.0 + (a - 1.0) * ka_ref[:, ln])
        bonus_parts.append(_mm(r * k2 * rk_ref[:, ln], headswap_ref[...]) * v)

        gcum = jnp.concatenate(
            [_mm_exact_lhs(cumtri_ref[...], logw[hb:hb + PAIR], 2) for hb in range(0, ROWS, PAIR)],
            axis=0)
        e_inv = jnp.exp(-gcum)
        kka = kkn * a
        at_s[:, ln] = -kkn * jnp.exp(gcum - logw)
        rt_s[:, ln] = r * jnp.exp(gcum)
        bt_s[:, ln] = kka * e_inv
        kt_s[:, ln] = k2 * e_inv
        v_s[:, ln] = v
        for c in range(DUO * N_CHUNKS):
            rows = slice(c * CHUNK, (c + 1) * CHUNK)
            gam_c = jnp.exp(gcum[rows.stop - 1:rows.stop, :])
            ge_s[c:c + 1, ln] = gam_c
            e_end = e_inv[rows] * gam_c
            bh_s[rows, ln] = kka[rows] * e_end
            kh_s[rows, ln] = k2[rows] * e_end
    bonus = jnp.concatenate(bonus_parts, axis=1)
    g_rest = jnp.dot(h, win_ref[:, lr0 + 2 * PAIR:], preferred_element_type=F32)
    gate_b = _silu(jnp.concatenate([z_lr[:, PAIR:], g_rest], axis=1))

    ri = lax.broadcasted_iota(jnp.int32, (CHUNK, PAIR), 0)
    ci = lax.broadcasted_iota(jnp.int32, (CHUNK, PAIR), 1)
    cm = ci % CHUNK
    left = ci < CHUNK
    strict_k = (ri > cm) & ~left
    strict_b = (ri > cm) & left
    incl = ri >= cm
    eye_r = jnp.where(ci == ri + CHUNK, 1.0, 0.0)
    lane_half = lax.broadcasted_iota(jnp.int32, (1, PAIR), 1) // HEAD
    key_mask = [lane_half == e for e in range(2)]
    eye_key = [ci == ri + e * HEAD for e in range(2)]
    zeros_blk = jnp.zeros((HEAD, PAIR), F32)
    keep_right = jnp.where(lane_half == 1, 1.0, 0.0)
    n_doubling = int(math.log2(CHUNK))
    heads = [(p, e) for p in range(N_HEADS // 2) for e in range(2)]
    chains = [(c, p, e) for c in range(DUO * N_CHUNKS) for p, e in heads]
    blk = lambda ref, c, p: ref[c * CHUNK:(c + 1) * CHUNK, p * PAIR:(p + 1) * PAIR]

    aall, vm, z, ry, tx, mgk = [], [], [], [], [], []
    for c, p, e in chains:
        if e == 0:
            lq = jnp.concatenate([blk(at_s, c, p), blk(rt_s, c, p)], axis=0)
            rk_t = jnp.concatenate([blk(bt_s, c, p), blk(kt_s, c, p)], axis=0).T
            rk_pair = jnp.concatenate(
                [jnp.concatenate([rk_t[:HEAD], zeros_blk], axis=1),
                 jnp.concatenate([zeros_blk, rk_t[HEAD:]], axis=1)], axis=0)
            scores = _mm(lq, rk_pair)
        aall.append(scores[:, e * PAIR:(e + 1) * PAIR])
        vm.append(jnp.where(key_mask[e], 0.0, blk(v_s, c, p)))
    for i in range(len(chains)):
        tx.append(jnp.where(strict_b, aall[i][:CHUNK], eye_r))
    per_level = -(-len(chains) // n_doubling)
    for level in range(n_doubling):
        for i in range(len(chains)):
            tx[i] = tx[i] * keep_right + _mm(tx[i][:, :CHUNK], tx[i])
        for i, (c, p, e) in list(enumerate(chains))[level * per_level:(level + 1) * per_level]:
            a_ak = jnp.where(strict_k, aall[i][:CHUNK], 0.0)
            av = _mm(a_ak, jnp.concatenate([vm[i], vm[i]], axis=0))
            z.append(jnp.where(key_mask[e], blk(at_s, c, p), av))
    for i in range(len(chains)):
        z[i] = _mm(tx[i], jnp.concatenate([z[i], z[i]], axis=0))
    for i, (c, p, e) in enumerate(chains):
        a_r = jnp.where(incl, aall[i][CHUNK:], 0.0)
        if e == 0:
            bk_t = jnp.concatenate([blk(bh_s, c, p), blk(kh_s, c, p)], axis=0).T
        both = _mm(jnp.concatenate([a_r, bk_t[e * HEAD:(e + 1) * HEAD]], axis=0),
                   jnp.concatenate([z[i], vm[i]], axis=0))
        ry.append(jnp.where(key_mask[e], blk(rt_s, c, p), 0.0) + both[:CHUNK])
        gam_c = ge_s[c:c + 1, p * PAIR:(p + 1) * PAIR]
        mgk.append(both[CHUNK:] + jnp.where(eye_key[e], gam_c, 0.0))
    y_blocks = [None] * (DUO * N_CHUNKS)
    for c, s in [(c, s) for c in range(N_CHUNKS) for s in range(DUO)]:
        ys = []
        for p in range(N_HEADS // 2):
            i0 = (s * N_CHUNKS + c) * N_HEADS + 2 * p
            sp = state[s, p]
            rhs = jnp.concatenate([jnp.where(key_mask[0], 0.0, sp),
                                   jnp.where(key_mask[0], sp, 0.0)], axis=0)
            ry_pair = jnp.where(key_mask[0], ry[i0], ry[i0 + 1])
            mgk_pair = jnp.where(key_mask[0], mgk[i0], mgk[i0 + 1])
            res = _mm(jnp.concatenate([ry_pair, mgk_pair], axis=0), rhs)
            ys.append(res[:CHUNK] + jnp.where(key_mask[0], ry[i0 + 1], ry[i0]))
            state[s, p] = res[CHUNK:] + jnp.where(key_mask[0], mgk[i0 + 1], mgk[i0])
        y_blocks[s * N_CHUNKS + c] = jnp.concatenate(ys, axis=1)
    y = jnp.concatenate(y_blocks, axis=0)

    mean = _pair_sum(y, headsum) * (1.0 / HEAD)
    d = y - mean
    var = _pair_sum(d * d, headsum) * (1.0 / HEAD)
    y = d * lax.rsqrt(var + GN_EPS) * gng_ref[...] + gnb_ref[...]
    y_b = (y + bonus) * gate_b

    yy = jnp.concatenate([y_a, y_b], axis=1).astype(BF16)
    xo = x + jnp.dot(yy, wout_ref[...], preferred_element_type=F32)
    inv_o = lax.rsqrt(jnp.mean(xo * xo, axis=-1, keepdims=True) + NORM_EPS)
    out = xo * inv_o * fg_ref[...]
    for s in range(DUO):
        o_ref[s, tile_rows, :] = out[s * TILE:(s + 1) * TILE]


def _block_kernel(x_ref, ng_ref, win_f, poolw_f, pscale_ref, mu_f, w0_ref, wup_f,
                  a0_ref, aup_f, kk_ref, ka_ref, rk_ref, gng_f, gnb_f, wout_f, fg_ref,
                  headsum_ref, headswap_ref, cumtri_ref,
                  o_ref,
                  shift_carry, pool_carry, state,
                  at_s, rt_s, bt_s, kt_s, bh_s, kh_s, v_s, ge_s,
                  win_ref, poolw_ref, mu_ref, lora_ref, gng_ref, gnb_ref, wout_ref):
    step = pl.program_id(1)

    @pl.when((pl.program_id(0) == 0) & (step == 0))
    def _():
        _prepare_weights(win_f, poolw_f, mu_f, wup_f, aup_f, gng_f, gnb_f, wout_f,
                         win_ref, poolw_ref, mu_ref, lora_ref, gng_ref, gnb_ref, wout_ref)

    def body(i, carry):
        _tile(step * STEP_TILES + i, pl.ds(pl.multiple_of(i * TILE, TILE), TILE),
              x_ref, ng_ref, pscale_ref, w0_ref, a0_ref, kk_ref, ka_ref, rk_ref, fg_ref,
              headsum_ref, headswap_ref, cumtri_ref, o_ref, shift_carry, pool_carry, state,
              at_s, rt_s, bt_s, kt_s, bh_s, kh_s, v_s, ge_s,
              win_ref, poolw_ref, mu_ref, lora_ref, gng_ref, gnb_ref, wout_ref)
        return carry

    lax.fori_loop(0, STEP_TILES, body, 0)


def _full(shape, single_buffer=False):
    index_map = lambda b, j: (0,) * len(shape)
    if single_buffer:
        return pl.BlockSpec(shape, index_map, pipeline_mode=pl.Buffered(1))
    return pl.BlockSpec(shape, index_map)


def kernel(x, norm_gain, w_in, pool_w, pool_scale, shift_mu, w0, w_up, a0, a_up, k_k, k_a, r_k,
           gn_gain, gn_bias, w_out, final_gain):
    batch, seq, d_model = x.shape
    step_rows = STEP_TILES * TILE
    assert norm_gain.shape[0] == 1 and d_model == D_MODEL and seq % step_rows == 0
    assert batch % DUO == 0
    assert w_in.shape == (1, D_MODEL, D_IN)

    hid = jnp.arange(PAIR) // HEAD
    headsum = (hid[:, None] == hid[None, :]).astype(BF16)
    headsum_swap = (hid[:, None] != hid[None, :]).astype(BF16)
    t = jnp.arange(PAIR)
    same_chunk = (t[:, None] // CHUNK) == (t[None, :] // CHUNK)
    cumtri = (same_chunk & (t[:, None] >= t[None, :])).astype(BF16)

    operands = (
        x, norm_gain, w_in, pool_w, pool_scale, shift_mu, w0, w_up, a0, a_up, k_k, k_a,
        r_k.reshape(1, D_RWKV), gn_gain, gn_bias, w_out, final_gain.reshape(1, D_MODEL),
        headsum, headsum_swap, cumtri,
    )
    in_specs = [pl.BlockSpec((DUO, step_rows, D_MODEL), lambda b, j: (b, j, 0))]
    in_specs += [_full(op.shape, single_buffer=op is w_in or op is w_out) for op in operands[1:]]
    seq_buf = lambda: pltpu.VMEM((ROWS, D_RWKV), F32)
    return pl.pallas_call(
        _block_kernel,
        grid=(batch // DUO, seq // step_rows),
        in_specs=in_specs,
        out_specs=pl.BlockSpec((DUO, step_rows, D_MODEL), lambda b, j: (b, j, 0)),
        out_shape=jax.ShapeDtypeStruct(x.shape, x.dtype),
        scratch_shapes=[
            pltpu.VMEM((DUO, D_SHIFT), F32),
            pltpu.VMEM((DUO, POOL_HALO, D_POOL), F32),
            pltpu.VMEM((DUO, N_HEADS // 2, HEAD, PAIR), F32),
        ] + [seq_buf() for _ in range(7)] + [
            pltpu.VMEM((max(DUO * N_CHUNKS, SUBLANES), D_RWKV), F32),
            pltpu.VMEM((D_MODEL, D_IN), BF16),
            pltpu.VMEM((len(POOL_WINDOWS), POOL_GROUP, POOL_GROUP), BF16),
            pltpu.VMEM((1, D_SHIFT), F32),
            pltpu.VMEM((2 * LORA, 2 * D_RWKV), BF16),
            pltpu.VMEM((1, D_RWKV), F32),
            pltpu.VMEM((1, D_RWKV), F32),
            pltpu.VMEM((D_MODEL, D_MODEL), BF16),
        ],
        compiler_params=pltpu.CompilerParams(
            dimension_semantics=("arbitrary", "arbitrary"),
            vmem_limit_bytes=VMEM_LIMIT_BYTES),
        name="hybrid_pool_rwkv7_block",
    )(*operands)
```

```python
import math

import jax
import jax.numpy as jnp
from jax import lax
from jax.experimental import pallas as pl
from jax.experimental.pallas import tpu as pltpu

F32 = jnp.float32
BF16 = jnp.bfloat16

D_MODEL = 1024
D_POOL = 512
POOL_WINDOWS = (2, 4, 8, 16)
POOL_GROUP = 128
POOL_HALO = 16
D_RWKV = 512
N_HEADS = 8
HEAD = 64
PAIR = 2 * HEAD
LORA = 64
D_SHIFT = 3 * D_RWKV + 2 * LORA
D_IN = 2 * D_POOL + D_SHIFT + D_RWKV
NORM_EPS = 1e-6
GN_EPS = 64e-5
KK_NORM_FLOOR = 1e-12
SUBLANES = 8
CHUNK = 64
TILE = 256
DUO = 2
ROWS = DUO * TILE
STEP_TILES = 1
N_CHUNKS = TILE // CHUNK
DECAY_SCALE = math.exp(-0.5)
VMEM_LIMIT_BYTES = 56 * 1024 * 1024


def _mm(a, b):
    return jnp.dot(a.astype(BF16), b.astype(BF16), preferred_element_type=F32)


def _split(a, parts):
    out = []
    rem = a
    for _ in range(parts):
        p = rem.astype(BF16)
        out.append(p)
        rem = rem - p.astype(F32)
    return out


def _mm_exact_lhs(a_bf16, b, parts):
    acc = None
    for p in _split(b, parts):
        t = jnp.dot(a_bf16, p, preferred_element_type=F32)
        acc = t if acc is None else acc + t
    return acc


def _pair_sum(a, ones_blk):
    return jnp.concatenate([_mm(a[:, c:c + PAIR], ones_blk) for c in range(0, a.shape[1], PAIR)],
                           axis=1)


def _sigmoid(t):
    return 0.5 * jnp.tanh(0.5 * t) + 0.5


def _silu(t):
    return t * _sigmoid(t)


def _swap_pair_lanes(a):
    rows = a.shape[0]
    if rows < SUBLANES:
        a = jnp.broadcast_to(a[0:1], (SUBLANES, a.shape[1]))
    out = jnp.concatenate([pltpu.roll(a[:, c:c + PAIR], HEAD, axis=1)
                           for c in range(0, a.shape[1], PAIR)], axis=1)
    return out[:rows]


def _prepare_weights(win_f, poolw_f, mu_f, wup_f, aup_f, gng_f, gnb_f, wout_f,
                     win_ref, poolw_ref, mu_ref, lora_ref, gng_ref, gnb_ref, wout_ref):
    v0 = 2 * D_POOL + 2 * D_RWKV
    g0 = 2 * D_POOL + D_SHIFT
    for c in range(0, D_IN, PAIR):
        w = win_f[0, :, c:c + PAIR]
        if v0 <= c < v0 + D_RWKV or c >= g0:
            w = pltpu.roll(w, HEAD, axis=1)
        win_ref[:, c:c + PAIR] = w.astype(BF16)
    wout_ref[0:D_POOL, :] = wout_f[0, 0:D_POOL, :].astype(BF16)
    for r in range(D_POOL, D_MODEL, PAIR):
        wout_ref[r:r + HEAD, :] = wout_f[0, r + HEAD:r + PAIR, :].astype(BF16)
        wout_ref[r + HEAD:r + PAIR, :] = wout_f[0, r:r + HEAD, :].astype(BF16)
    poolw_ref[...] = poolw_f[0].astype(BF16)
    lora_ref[...] = jnp.zeros_like(lora_ref)
    lora_ref[0:LORA, 0:D_RWKV] = wup_f[0].astype(BF16)
    lora_ref[LORA:2 * LORA, D_RWKV:2 * D_RWKV] = aup_f[0].astype(BF16)
    mu_ref[...] = mu_f[...]
    mu_ref[:, 2 * D_RWKV:3 * D_RWKV] = _swap_pair_lanes(mu_f[:, 2 * D_RWKV:3 * D_RWKV])
    gng_ref[...] = _swap_pair_lanes(gng_f[...])
    gnb_ref[...] = _swap_pair_lanes(gnb_f[...])


def _tile(j, tile_rows, x_ref, ng_ref, pscale_ref, w0_ref, a0_ref, kk_ref, ka_ref, rk_ref, fg_ref,
          headsum_ref, headswap_ref, cumtri_ref, o_ref, shift_carry, pool_carry, state,
          at_s, rt_s, bt_s, kt_s, bh_s, kh_s, v_s, ge_s,
          win_ref, poolw_ref, mu_ref, lora_ref, gng_ref, gnb_ref, wout_ref):
    @pl.when(j == 0)
    def _():
        shift_carry[...] = jnp.zeros_like(shift_carry)
        pool_carry[...] = jnp.zeros_like(pool_carry)
        state[...] = jnp.zeros_like(state)

    x = jnp.concatenate([x_ref[s, tile_rows, :] for s in range(DUO)], axis=0)
    inv = lax.rsqrt(jnp.mean(x * x, axis=-1, keepdims=True) + NORM_EPS)
    h = (x * inv * ng_ref[...]).astype(BF16)

    row = lax.broadcasted_iota(jnp.int32, (ROWS, 1), 0) % TILE

    def token_shift(z, cols):
        prev = pltpu.roll(z, 1, axis=0)
        parts = []
        for s in range(DUO):
            r0 = s * TILE
            parts.append(jnp.where(row[:SUBLANES] == 0, shift_carry[s:s + 1, cols],
                                   prev[r0:r0 + SUBLANES]))
            parts.append(prev[r0 + SUBLANES:r0 + TILE])
            shift_carry[s:s + 1, cols] = z[r0 + TILE - 1:r0 + TILE, :]
        return z + mu_ref[:, cols] * (jnp.concatenate(parts, axis=0) - z)

    lr0 = 2 * D_POOL + 3 * D_RWKV
    z_lr = jnp.dot(h, win_ref[:, lr0:lr0 + 2 * PAIR], preferred_element_type=F32)
    z_rkv = jnp.dot(h, win_ref[:, 2 * D_POOL:lr0], preferred_element_type=F32)
    lr = token_shift(z_lr[:, :PAIR], slice(3 * D_RWKV, D_SHIFT))
    lane = lax.broadcasted_iota(jnp.int32, (1, 2 * LORA), 1)
    lr = jnp.where(lane < LORA, jnp.tanh(lr), lr)
    lora = _mm(lr, lora_ref[...])
    sh = token_shift(z_rkv, slice(0, 3 * D_RWKV))

    z_pool = jnp.dot(h, win_ref[:, 0:2 * D_POOL], preferred_element_type=F32)
    u = z_pool[:, :D_POOL]
    g_a = z_pool[:, D_POOL:]
    ext = jnp.concatenate(
        [blk for s in range(DUO) for blk in (pool_carry[s], u[s * TILE:(s + 1) * TILE])], axis=0)
    for s in range(DUO):
        pool_carry[s] = u[(s + 1) * TILE - POOL_HALO:(s + 1) * TILE, :]
    seq_rows = lambda t: jnp.concatenate(
        [t[s * (POOL_HALO + TILE) + POOL_HALO:(s + 1) * (POOL_HALO + TILE)] for s in range(DUO)],
        axis=0)
    pos = (j * TILE + row).astype(F32)
    pooled = []
    s = ext
    for g, w in enumerate(POOL_WINDOWS):
        lo = g * POOL_GROUP
        s = s[:, (POOL_GROUP if g > 0 else 0):]
        s = s + pltpu.roll(s, w // 2, axis=0)
        inv_cnt = 1.0 / jnp.minimum(pos + 1.0, float(w))
        pooled.append(seq_rows(s[:, :POOL_GROUP]) * inv_cnt - u[:, lo:lo + POOL_GROUP])
    mixed = [_mm(pooled[g], poolw_ref[g]) for g in range(len(POOL_WINDOWS))]
    y_a = jnp.concatenate(mixed, axis=1) * pscale_ref[...] * _silu(g_a)

    headsum = headsum_ref[...]
    bonus_parts = []
    for p in range(N_HEADS // 2):
        ln = slice(p * PAIR, (p + 1) * PAIR)
        off = lambda base: slice(base + p * PAIR, base + (p + 1) * PAIR)
        r = sh[:, ln]
        k = sh[:, off(D_RWKV)]
        v = sh[:, off(2 * D_RWKV)]
        ww = w0_ref[:, ln] + lora[:, ln]
        a = _sigmoid(a0_ref[:, ln] + lora[:, off(D_RWKV)])
        logw = -DECAY_SCALE * _sigmoid(ww)

        kkv = k * kk_ref[:, ln]
        ss = _mm(kkv * kkv, headsum)
        kkn = kkv * lax.rsqrt(jnp.maximum(ss, KK_NORM_FLOOR ** 2))
        k2 = k * (1.0 + (a - 1.0) * ka_ref[:, ln])
        bonus_parts.append(_mm(r * k2 * rk_ref[:, ln], headswap_ref[...]) * v)

        gcum = jnp.concatenate(
            [_mm_exact_lhs(cumtri_ref[...], logw[hb:hb + PAIR], 2) for hb in range(0, ROWS, PAIR)],
            axis=0)
        e_inv = jnp.exp(-gcum)
        kka = kkn * a
        at_s[:, ln] = -kkn * jnp.exp(gcum - logw)
        rt_s[:, ln] = r * jnp.exp(gcum)
        bt_s[:, ln] = kka * e_inv
        kt_s[:, ln] = k2 * e_inv
        v_s[:, ln] = v
        for c in range(DUO * N_CHUNKS):
            rows = slice(c * CHUNK, (c + 1) * CHUNK)
            gam_c = jnp.exp(gcum[rows.stop - 1:rows.stop, :])
            ge_s[c:c + 1, ln] = gam_c
            e_end = e_inv[rows] * gam_c
            bh_s[rows, ln] = kka[rows] * e_end
            kh_s[rows, ln] = k2[rows] * e_end
    bonus = jnp.concatenate(bonus_parts, axis=1)
    g_rest = jnp.dot(h, win_ref[:, lr0 + 2 * PAIR:], preferred_element_type=F32)
    gate_b = _silu(jnp.concatenate([z_lr[:, PAIR:], g_rest], axis=1))

    ri = lax.broadcasted_iota(jnp.int32, (CHUNK, PAIR), 0)
    ci = lax.broadcasted_iota(jnp.int32, (CHUNK, PAIR), 1)
    cm = ci % CHUNK
    left = ci < CHUNK
    strict_k = (ri > cm) & ~left
    strict_b = (ri > cm) & left
    incl = ri >= cm
    eye_r = jnp.where(ci == ri + CHUNK, 1.0, 0.0)
    lane_half = lax.broadcasted_iota(jnp.int32, (1, PAIR), 1) // HEAD
    key_mask = [lane_half == e for e in range(2)]
    eye_key = [ci == ri + e * HEAD for e in range(2)]
    zeros_blk = jnp.zeros((HEAD, PAIR), F32)
    keep_right = jnp.where(lane_half == 1, 1.0, 0.0)
    n_doubling = int(math.log2(CHUNK))
    heads = [(p, e) for p in range(N_HEADS // 2) for e in range(2)]
    chains = [(c, p, e) for c in range(DUO * N_CHUNKS) for p, e in heads]
    blk = lambda ref, c, p: ref[c * CHUNK:(c + 1) * CHUNK, p * PAIR:(p + 1) * PAIR]

    aall, vm, z, ry, tx, mgk = [], [], [], [], [], []
    for c, p, e in chains:
        if e == 0:
            lq = jnp.concatenate([blk(at_s, c, p), blk(rt_s, c, p)], axis=0)
            rk_t = jnp.concatenate([blk(bt_s, c, p), blk(kt_s, c, p)], axis=0).T
            rk_pair = jnp.concatenate(
                [jnp.concatenate([rk_t[:HEAD], zeros_blk], axis=1),
                 jnp.concatenate([zeros_blk, rk_t[HEAD:]], axis=1)], axis=0)
            scores = _mm(lq, rk_pair)
        aall.append(scores[:, e * PAIR:(e + 1) * PAIR])
        vm.append(jnp.where(key_mask[e], 0.0, blk(v_s, c, p)))
    for i in range(len(chains)):
        tx.append(jnp.where(strict_b, aall[i][:CHUNK], eye_r))
    per_level = -(-len(chains) // n_doubling)
    for level in range(n_doubling):
        for i in range(len(chains)):
            tx[i] = tx[i] * keep_right + _mm(tx[i][:, :CHUNK], tx[i])
        for i, (c, p, e) in list(enumerate(chains))[level * per_level:(level + 1) * per_level]:
            a_ak = jnp.where(strict_k, aall[i][:CHUNK], 0.0)
            av = _mm(a_ak, jnp.concatenate([vm[i], vm[i]], axis=0))
            z.append(jnp.where(key_mask[e], blk(at_s, c, p), av))
    for i in range(len(chains)):
        z[i] = _mm(tx[i], jnp.concatenate([z[i], z[i]], axis=0))
    for i, (c, p, e) in enumerate(chains):
        a_r = jnp.where(incl, aall[i][CHUNK:], 0.0)
        if e == 0:
            bk_t = jnp.concatenate([blk(bh_s, c, p), blk(kh_s, c, p)], axis=0).T
        both = _mm(jnp.concatenate([a_r, bk_t[e * HEAD:(e + 1) * HEAD]], axis=0),
                   jnp.concatenate([z[i], vm[i]], axis=0))
        ry.append(jnp.where(key_mask[e], blk(rt_s, c, p), 0.0) + both[:CHUNK])
        gam_c = ge_s[c:c + 1, p * PAIR:(p + 1) * PAIR]
        mgk.append(both[CHUNK:] + jnp.where(eye_key[e], gam_c, 0.0))
    y_blocks = [None] * (DUO * N_CHUNKS)
    for c, s in [(c, s) for c in range(N_CHUNKS) for s in range(DUO)]:
        ys = []
        for p in range(N_HEADS // 2):
            i0 = (s * N_CHUNKS + c) * N_HEADS + 2 * p
            sp = state[s, p]
            rhs = jnp.concatenate([jnp.where(key_mask[0], 0.0, sp),
                                   jnp.where(key_mask[0], sp, 0.0)], axis=0)
            ry_pair = jnp.where(key_mask[0], ry[i0], ry[i0 + 1])
            mgk_pair = jnp.where(key_mask[0], mgk[i0], mgk[i0 + 1])
            res = _mm(jnp.concatenate([ry_pair, mgk_pair], axis=0), rhs)
            ys.append(res[:CHUNK] + jnp.where(key_mask[0], ry[i0 + 1], ry[i0]))
            state[s, p] = res[CHUNK:] + jnp.where(key_mask[0], mgk[i0 + 1], mgk[i0])
        y_blocks[s * N_CHUNKS + c] = jnp.concatenate(ys, axis=1)
    y = jnp.concatenate(y_blocks, axis=0)

    mean = _pair_sum(y, headsum) * (1.0 / HEAD)
    d = y - mean
    var = _pair_sum(d * d, headsum) * (1.0 / HEAD)
    y = d * lax.rsqrt(var + GN_EPS) * gng_ref[...] + gnb_ref[...]
    y_b = (y + bonus) * gate_b

    yy = jnp.concatenate([y_a, y_b], axis=1).astype(BF16)
    xo = x + jnp.dot(yy, wout_ref[...], preferred_element_type=F32)
    inv_o = lax.rsqrt(jnp.mean(xo * xo, axis=-1, keepdims=True) + NORM_EPS)
    out = xo * inv_o * fg_ref[...]
    for s in range(DUO):
        o_ref[s, tile_rows, :] = out[s * TILE:(s + 1) * TILE]


def _block_kernel(x_ref, ng_ref, win_f, poolw_f, pscale_ref, mu_f, w0_ref, wup_f,
                  a0_ref, aup_f, kk_ref, ka_ref, rk_ref, gng_f, gnb_f, wout_f, fg_ref,
                  headsum_ref, headswap_ref, cumtri_ref,
                  o_ref,
                  shift_carry, pool_carry, state,
                  at_s, rt_s, bt_s, kt_s, bh_s, kh_s, v_s, ge_s,
                  win_ref, poolw_ref, mu_ref, lora_ref, gng_ref, gnb_ref, wout_ref):
    step = pl.program_id(1)

    @pl.when((pl.program_id(0) == 0) & (step == 0))
    def _():
        _prepare_weights(win_f, poolw_f, mu_f, wup_f, aup_f, gng_f, gnb_f, wout_f,
                         win_ref, poolw_ref, mu_ref, lora_ref, gng_ref, gnb_ref, wout_ref)

    def body(i, carry):
        _tile(step * STEP_TILES + i, pl.ds(pl.multiple_of(i * TILE, TILE), TILE),
              x_ref, ng_ref, pscale_ref, w0_ref, a0_ref, kk_ref, ka_ref, rk_ref, fg_ref,
              headsum_ref, headswap_ref, cumtri_ref, o_ref, shift_carry, pool_carry, state,
              at_s, rt_s, bt_s, kt_s, bh_s, kh_s, v_s, ge_s,
              win_ref, poolw_ref, mu_ref, lora_ref, gng_ref, gnb_ref, wout_ref)
        return carry

    lax.fori_loop(0, STEP_TILES, body, 0)


def _full(shape, single_buffer=False):
    index_map = lambda b, j: (0,) * len(shape)
    if single_buffer:
        return pl.BlockSpec(shape, index_map, pipeline_mode=pl.Buffered(1))
    return pl.BlockSpec(shape, index_map)


def kernel(x, norm_gain, w_in, pool_w, pool_scale, shift_mu, w0, w_up, a0, a_up, k_k, k_a, r_k,
           gn_gain, gn_bias, w_out, final_gain):
    batch, seq, d_model = x.shape
    step_rows = STEP_TILES * TILE
    assert norm_gain.shape[0] == 1 and d_model == D_MODEL and seq % step_rows == 0
    assert batch % DUO == 0
    assert w_in.shape == (1, D_MODEL, D_IN)

    hid = jnp.arange(PAIR) // HEAD
    headsum = (hid[:, None] == hid[None, :]).astype(BF16)
    headsum_swap = (hid[:, None] != hid[None, :]).astype(BF16)
    t = jnp.arange(PAIR)
    same_chunk = (t[:, None] // CHUNK) == (t[None, :] // CHUNK)
    cumtri = (same_chunk & (t[:, None] >= t[None, :])).astype(BF16)

    operands = (
        x, norm_gain, w_in, pool_w, pool_scale, shift_mu, w0, w_up, a0, a_up, k_k, k_a,
        r_k.reshape(1, D_RWKV), gn_gain, gn_bias, w_out, final_gain.reshape(1, D_MODEL),
        headsum, headsum_swap, cumtri,
    )
    in_specs = [pl.BlockSpec((DUO, step_rows, D_MODEL), lambda b, j: (b, j, 0))]
    in_specs += [_full(op.shape, single_buffer=op is w_in or op is w_out) for op in operands[1:]]
    seq_buf = lambda: pltpu.VMEM((ROWS, D_RWKV), F32)
    return pl.pallas_call(
        _block_kernel,
        grid=(batch // DUO, seq // step_rows),
        in_specs=in_specs,
        out_specs=pl.BlockSpec((DUO, step_rows, D_MODEL), lambda b, j: (b, j, 0)),
        out_shape=jax.ShapeDtypeStruct(x.shape, x.dtype),
        scratch_shapes=[
            pltpu.VMEM((DUO, D_SHIFT), F32),
            pltpu.VMEM((DUO, POOL_HALO, D_POOL), F32),
            pltpu.VMEM((DUO, N_HEADS // 2, HEAD, PAIR), F32),
        ] + [seq_buf() for _ in range(7)] + [
            pltpu.VMEM((max(DUO * N_CHUNKS, SUBLANES), D_RWKV), F32),
            pltpu.VMEM((D_MODEL, D_IN), BF16),
            pltpu.VMEM((len(POOL_WINDOWS), POOL_GROUP, POOL_GROUP), BF16),
            pltpu.VMEM((1, D_SHIFT), F32),
            pltpu.VMEM((2 * LORA, 2 * D_RWKV), BF16),
            pltpu.VMEM((1, D_RWKV), F32),
            pltpu.VMEM((1, D_RWKV), F32),
            pltpu.VMEM((D_MODEL, D_MODEL), BF16),
        ],
        compiler_params=pltpu.CompilerParams(
            dimension_semantics=("arbitrary", "arbitrary"),
            vmem_limit_bytes=VMEM_LIMIT_BYTES),
        name="hybrid_pool_rwkv7_block",
    )(*operands)
```

```python
import math

import jax
import jax.numpy as jnp
from jax import lax
from jax.experimental import pallas as pl
from jax.experimental.pallas import tpu as pltpu

F32 = jnp.float32
BF16 = jnp.bfloat16

D_MODEL = 1024
D_POOL = 512
POOL_WINDOWS = (2, 4, 8, 16)
POOL_GROUP = 128
POOL_HALO = 16
D_RWKV = 512
N_HEADS = 8
HEAD = 64
PAIR = 2 * HEAD
LORA = 64
D_SHIFT = 3 * D_RWKV + 2 * LORA
D_IN = 2 * D_POOL + D_SHIFT + D_RWKV
NORM_EPS = 1e-6
GN_EPS = 64e-5
KK_NORM_FLOOR = 1e-12
SUBLANES = 8
CHUNK = 64
TILE = 256
DUO = 2
ROWS = DUO * TILE
STEP_TILES = 1
N_CHUNKS = TILE // CHUNK
DECAY_SCALE = math.exp(-0.5)
VMEM_LIMIT_BYTES = 56 * 1024 * 1024


def _mm(a, b):
    return jnp.dot(a.astype(BF16), b.astype(BF16), preferred_element_type=F32)


def _split(a, parts):
    out = []
    rem = a
    for _ in range(parts):
        p = rem.astype(BF16)
        out.append(p)
        rem = rem - p.astype(F32)
    return out


def _mm_exact_lhs(a_bf16, b, parts):
    acc = None
    for p in _split(b, parts):
        t = jnp.dot(a_bf16, p, preferred_element_type=F32)
        acc = t if acc is None else acc + t
    return acc


def _pair_sum(a, ones_blk):
    return jnp.concatenate([_mm(a[:, c:c + PAIR], ones_blk) for c in range(0, a.shape[1], PAIR)],
                           axis=1)


def _sigmoid(t):
    return 0.5 * jnp.tanh(0.5 * t) + 0.5


def _silu(t):
    return t * _sigmoid(t)


def _swap_pair_lanes(a):
    rows = a.shape[0]
    if rows < SUBLANES:
        a = jnp.broadcast_to(a[0:1], (SUBLANES, a.shape[1]))
    out = jnp.concatenate([pltpu.roll(a[:, c:c + PAIR], HEAD, axis=1)
                           for c in range(0, a.shape[1], PAIR)], axis=1)
    return out[:rows]


def _prepare_weights(win_f, poolw_f, mu_f, wup_f, aup_f, gng_f, gnb_f, wout_f,
                     win_ref, poolw_ref, mu_ref, lora_ref, gng_ref, gnb_ref, wout_ref):
    v0 = 2 * D_POOL + 2 * D_RWKV
    g0 = 2 * D_POOL + D_SHIFT
    for c in range(0, D_IN, PAIR):
        w = win_f[0, :, c:c + PAIR]
        if v0 <= c < v0 + D_RWKV or c >= g0:
            w = pltpu.roll(w, HEAD, axis=1)
        win_ref[:, c:c + PAIR] = w.astype(BF16)
    wout_ref[0:D_POOL, :] = wout_f[0, 0:D_POOL, :].astype(BF16)
    for r in range(D_POOL, D_MODEL, PAIR):
        wout_ref[r:r + HEAD, :] = wout_f[0, r + HEAD:r + PAIR, :].astype(BF16)
        wout_ref[r + HEAD:r + PAIR, :] = wout_f[0, r:r + HEAD, :].astype(BF16)
    poolw_ref[...] = poolw_f[0].astype(BF16)
    lora_ref[...] = jnp.zeros_like(lora_ref)
    lora_ref[0:LORA, 0:D_RWKV] = wup_f[0].astype(BF16)
    lora_ref[LORA:2 * LORA, D_RWKV:2 * D_RWKV] = aup_f[0].astype(BF16)
    mu_ref[...] = mu_f[...]
    mu_ref[:, 2 * D_RWKV:3 * D_RWKV] = _swap_pair_lanes(mu_f[:, 2 * D_RWKV:3 * D_RWKV])
    gng_ref[...] = _swap_pair_lanes(gng_f[...])
    gnb_ref[...] = _swap_pair_lanes(gnb_f[...])


def _tile(j, tile_rows, x_ref, ng_ref, pscale_ref, w0_ref, a0_ref, kk_ref, ka_ref, rk_ref, fg_ref,
          headsum_ref, headswap_ref, cumtri_ref, o_ref, shift_carry, pool_carry, state,
          at_s, rt_s, bt_s, kt_s, bh_s, kh_s, v_s, ge_s,
          win_ref, poolw_ref, mu_ref, lora_ref, gng_ref, gnb_ref, wout_ref):
    @pl.when(j == 0)
    def _():
        shift_carry[...] = jnp.zeros_like(shift_carry)
        pool_carry[...] = jnp.zeros_like(pool_carry)
        state[...] = jnp.zeros_like(state)

    x = jnp.concatenate([x_ref[s, tile_rows, :] for s in range(DUO)], axis=0)
    inv = lax.rsqrt(jnp.mean(x * x, axis=-1, keepdims=True) + NORM_EPS)
    h = (x * inv * ng_ref[...]).astype(BF16)

    row = lax.broadcasted_iota(jnp.int32, (ROWS, 1), 0) % TILE

    def token_shift(z, cols):
        prev = pltpu.roll(z, 1, axis=0)
        parts = []
        for s in range(DUO):
            r0 = s * TILE
            parts.append(jnp.where(row[:SUBLANES] == 0, shift_carry[s:s + 1, cols],
                                   prev[r0:r0 + SUBLANES]))
            parts.append(prev[r0 + SUBLANES:r0 + TILE])
            shift_carry[s:s + 1, cols] = z[r0 + TILE - 1:r0 + TILE, :]
        return z + mu_ref[:, cols] * (jnp.concatenate(parts, axis=0) - z)

    lr0 = 2 * D_POOL + 3 * D_RWKV
    z_lr = jnp.dot(h, win_ref[:, lr0:lr0 + 2 * PAIR], preferred_element_type=F32)
    z_rkv = jnp.dot(h, win_ref[:, 2 * D_POOL:lr0], preferred_element_type=F32)
    lr = token_shift(z_lr[:, :PAIR], slice(3 * D_RWKV, D_SHIFT))
    lane = lax.broadcasted_iota(jnp.int32, (1, 2 * LORA), 1)
    lr = jnp.where(lane < LORA, jnp.tanh(lr), lr)
    lora = _mm(lr, lora_ref[...])
    sh = token_shift(z_rkv, slice(0, 3 * D_RWKV))

    z_pool = jnp.dot(h, win_ref[:, 0:2 * D_POOL], preferred_element_type=F32)
    u = z_pool[:, :D_POOL]
    g_a = z_pool[:, D_POOL:]
    ext = jnp.concatenate(
        [blk for s in range(DUO) for blk in (pool_carry[s], u[s * TILE:(s + 1) * TILE])], axis=0)
    for s in range(DUO):
        pool_carry[s] = u[(s + 1) * TILE - POOL_HALO:(s + 1) * TILE, :]
    seq_rows = lambda t: jnp.concatenate(
        [t[s * (POOL_HALO + TILE) + POOL_HALO:(s + 1) * (POOL_HALO + TILE)] for s in range(DUO)],
        axis=0)
    pos = (j * TILE + row).astype(F32)
    pooled = []
    s = ext
    for g, w in enumerate(POOL_WINDOWS):
        lo = g * POOL_GROUP
        s = s[:, (POOL_GROUP if g > 0 else 0):]
        s = s + pltpu.roll(s, w // 2, axis=0)
        inv_cnt = 1.0 / jnp.minimum(pos + 1.0, float(w))
        pooled.append(seq_rows(s[:, :POOL_GROUP]) * inv_cnt - u[:, lo:lo + POOL_GROUP])
    mixed = [_mm(pooled[g], poolw_ref[g]) for g in range(len(POOL_WINDOWS))]
    y_a = jnp.concatenate(mixed, axis=1) * pscale_ref[...] * _silu(g_a)

    headsum = headsum_ref[...]
    bonus_parts = []
    for p in range(N_HEADS // 2):
        ln = slice(p * PAIR, (p + 1) * PAIR)
        off = lambda base: slice(base + p * PAIR, base + (p + 1) * PAIR)
        r = sh[:, ln]
        k = sh[:, off(D_RWKV)]
        v = sh[:, off(2 * D_RWKV)]
        ww = w0_ref[:, ln] + lora[:, ln]
        a = _sigmoid(a0_ref[:, ln] + lora[:, off(D_RWKV)])
        logw = -DECAY_SCALE * _sigmoid(ww)

        kkv = k * kk_ref[:, ln]
        ss = _mm(kkv * kkv, headsum)
        kkn = kkv * lax.rsqrt(jnp.maximum(ss, KK_NORM_FLOOR ** 2))
        k2 = k * (1.0 + (a - 1.0) * ka_ref[:, ln])
        bonus_parts.append(_mm(r * k2 * rk_ref[:, ln], headswap_ref[...]) * v)

        gcum = jnp.concatenate(
            [_mm_exact_lhs(cumtri_ref[...], logw[hb:hb + PAIR], 2) for hb in range(0, ROWS, PAIR)],
            axis=0)
        e_inv = jnp.exp(-gcum)
        kka = kkn * a
        at_s[:, ln] = -kkn * jnp.exp(gcum - logw)
        rt_s[:, ln] = r * jnp.exp(gcum)
        bt_s[:, ln] = kka * e_inv
        kt_s[:, ln] = k2 * e_inv
        v_s[:, ln] = v
        for c in range(DUO * N_CHUNKS):
            rows = slice(c * CHUNK, (c + 1) * CHUNK)
            gam_c = jnp.exp(gcum[rows.stop - 1:rows.stop, :])
            ge_s[c:c + 1, ln] = gam_c
            e_end = e_inv[rows] * gam_c
            bh_s[rows, ln] = kka[rows] * e_end
            kh_s[rows, ln] = k2[rows] * e_end
    bonus = jnp.concatenate(bonus_parts, axis=1)
    g_rest = jnp.dot(h, win_ref[:, lr0 + 2 * PAIR:], preferred_element_type=F32)
    gate_b = _silu(jnp.concatenate([z_lr[:, PAIR:], g_rest], axis=1))

    ri = lax.broadcasted_iota(jnp.int32, (CHUNK, PAIR), 0)
    ci = lax.broadcasted_iota(jnp.int32, (CHUNK, PAIR), 1)
    cm = ci % CHUNK
    left = ci < CHUNK
    strict_k = (ri > cm) & ~left
    strict_b = (ri > cm) & left
    incl = ri >= cm
    eye_r = jnp.where(ci == ri + CHUNK, 1.0, 0.0)
    lane_half = lax.broadcasted_iota(jnp.int32, (1, PAIR), 1) // HEAD
    key_mask = [lane_half == e for e in range(2)]
    eye_key = [ci == ri + e * HEAD for e in range(2)]
    zeros_blk = jnp.zeros((HEAD, PAIR), F32)
    keep_right = jnp.where(lane_half == 1, 1.0, 0.0)
    n_doubling = int(math.log2(CHUNK))
    heads = [(p, e) for p in range(N_HEADS // 2) for e in range(2)]
    chains = [(c, p, e) for c in range(DUO * N_CHUNKS) for p, e in heads]
    blk = lambda ref, c, p: ref[c * CHUNK:(c + 1) * CHUNK, p * PAIR:(p + 1) * PAIR]

    aall, vm, z, ry, tx, mgk = [], [], [], [], [], []
    for c, p, e in chains:
        if e == 0:
            lq = jnp.concatenate([blk(at_s, c, p), blk(rt_s, c, p)], axis=0)
            rk_t = jnp.concatenate([blk(bt_s, c, p), blk(kt_s, c, p)], axis=0).T
            rk_pair = jnp.concatenate(
                [jnp.concatenate([rk_t[:HEAD], zeros_blk], axis=1),
                 jnp.concatenate([zeros_blk, rk_t[HEAD:]], axis=1)], axis=0)
            scores = _mm(lq, rk_pair)
        aall.append(scores[:, e * PAIR:(e + 1) * PAIR])
        vm.append(jnp.where(key_mask[e], 0.0, blk(v_s, c, p)))
    for i in range(len(chains)):
        tx.append(jnp.where(strict_b, aall[i][:CHUNK], eye_r))
    for level in range(n_doubling):
        for i in range(len(chains)):
            tx[i] = tx[i] * keep_right + _mm(tx[i][:, :CHUNK], tx[i])
        if level == 0:
            for i, (c, p, e) in enumerate(chains):
                a_ak = jnp.where(strict_k, aall[i][:CHUNK], 0.0)
                av = _mm(a_ak, jnp.concatenate([vm[i], vm[i]], axis=0))
                z.append(jnp.where(key_mask[e], blk(at_s, c, p), av))
    for i in range(len(chains)):
        z[i] = _mm(tx[i], jnp.concatenate([z[i], z[i]], axis=0))
    for i, (c, p, e) in enumerate(chains):
        a_r = jnp.where(incl, aall[i][CHUNK:], 0.0)
        if e == 0:
            bk_t = jnp.concatenate([blk(bh_s, c, p), blk(kh_s, c, p)], axis=0).T
        both = _mm(jnp.concatenate([a_r, bk_t[e * HEAD:(e + 1) * HEAD]], axis=0),
                   jnp.concatenate([z[i], vm[i]], axis=0))
        ry.append(jnp.where(key_mask[e], blk(rt_s, c, p), 0.0) + both[:CHUNK])
        gam_c = ge_s[c:c + 1, p * PAIR:(p + 1) * PAIR]
        mgk.append(both[CHUNK:] + jnp.where(eye_key[e], gam_c, 0.0))
    y_blocks = [None] * (DUO * N_CHUNKS)
    for c, s in [(c, s) for c in range(N_CHUNKS) for s in range(DUO)]:
        ys = []
        for p in range(N_HEADS // 2):
            i0 = (s * N_CHUNKS + c) * N_HEADS + 2 * p
            sp = state[s, p]
            rhs = jnp.concatenate([jnp.where(key_mask[0], 0.0, sp),
                                   jnp.where(key_mask[0], sp, 0.0)], axis=0)
            ry_pair = jnp.where(key_mask[0], ry[i0], ry[i0 + 1])
            mgk_pair = jnp.where(key_mask[0], mgk[i0], mgk[i0 + 1])
            res = _mm(jnp.concatenate([ry_pair, mgk_pair], axis=0), rhs)
            ys.append(res[:CHUNK] + jnp.where(key_mask[0], ry[i0 + 1], ry[i0]))
            state[s, p] = res[CHUNK:] + jnp.where(key_mask[0], mgk[i0 + 1], mgk[i0])
        y_blocks[s * N_CHUNKS + c] = jnp.concatenate(ys, axis=1)
    y = jnp.concatenate(y_blocks, axis=0)

    mean = _pair_sum(y, headsum) * (1.0 / HEAD)
    d = y - mean
    var = _pair_sum(d * d, headsum) * (1.0 / HEAD)
    y = d * lax.rsqrt(var + GN_EPS) * gng_ref[...] + gnb_ref[...]
    y_b = (y + bonus) * gate_b

    yy = jnp.concatenate([y_a, y_b], axis=1).astype(BF16)
    xo = x + jnp.dot(yy, wout_ref[...], preferred_element_type=F32)
    inv_o = lax.rsqrt(jnp.mean(xo * xo, axis=-1, keepdims=True) + NORM_EPS)
    out = xo * inv_o * fg_ref[...]
    for s in range(DUO):
        o_ref[s, tile_rows, :] = out[s * TILE:(s + 1) * TILE]


def _block_kernel(x_ref, ng_ref, win_f, poolw_f, pscale_ref, mu_f, w0_ref, wup_f,
                  a0_ref, aup_f, kk_ref, ka_ref, rk_ref, gng_f, gnb_f, wout_f, fg_ref,
                  headsum_ref, headswap_ref, cumtri_ref,
                  o_ref,
                  shift_carry, pool_carry, state,
                  at_s, rt_s, bt_s, kt_s, bh_s, kh_s, v_s, ge_s,
                  win_ref, poolw_ref, mu_ref, lora_ref, gng_ref, gnb_ref, wout_ref):
    step = pl.program_id(1)

    @pl.when((pl.program_id(0) == 0) & (step == 0))
    def _():
        _prepare_weights(win_f, poolw_f, mu_f, wup_f, aup_f, gng_f, gnb_f, wout_f,
                         win_ref, poolw_ref, mu_ref, lora_ref, gng_ref, gnb_ref, wout_ref)

    def body(i, carry):
        _tile(step * STEP_TILES + i, pl.ds(pl.multiple_of(i * TILE, TILE), TILE),
              x_ref, ng_ref, pscale_ref, w0_ref, a0_ref, kk_ref, ka_ref, rk_ref, fg_ref,
              headsum_ref, headswap_ref, cumtri_ref, o_ref, shift_carry, pool_carry, state,
              at_s, rt_s, bt_s, kt_s, bh_s, kh_s, v_s, ge_s,
              win_ref, poolw_ref, mu_ref, lora_ref, gng_ref, gnb_ref, wout_ref)
        return carry

    lax.fori_loop(0, STEP_TILES, body, 0)


def _full(shape, single_buffer=False):
    index_map = lambda b, j: (0,) * len(shape)
    if single_buffer:
        return pl.BlockSpec(shape, index_map, pipeline_mode=pl.Buffered(1))
    return pl.BlockSpec(shape, index_map)


def kernel(x, norm_gain, w_in, pool_w, pool_scale, shift_mu, w0, w_up, a0, a_up, k_k, k_a, r_k,
           gn_gain, gn_bias, w_out, final_gain):
    batch, seq, d_model = x.shape
    step_rows = STEP_TILES * TILE
    assert norm_gain.shape[0] == 1 and d_model == D_MODEL and seq % step_rows == 0
    assert batch % DUO == 0
    assert w_in.shape == (1, D_MODEL, D_IN)

    hid = jnp.arange(PAIR) // HEAD
    headsum = (hid[:, None] == hid[None, :]).astype(BF16)
    headsum_swap = (hid[:, None] != hid[None, :]).astype(BF16)
    t = jnp.arange(PAIR)
    same_chunk = (t[:, None] // CHUNK) == (t[None, :] // CHUNK)
    cumtri = (same_chunk & (t[:, None] >= t[None, :])).astype(BF16)

    operands = (
        x, norm_gain, w_in, pool_w, pool_scale, shift_mu, w0, w_up, a0, a_up, k_k, k_a,
        r_k.reshape(1, D_RWKV), gn_gain, gn_bias, w_out, final_gain.reshape(1, D_MODEL),
        headsum, headsum_swap, cumtri,
    )
    in_specs = [pl.BlockSpec((DUO, step_rows, D_MODEL), lambda b, j: (b, j, 0))]
    in_specs += [_full(op.shape, single_buffer=op is w_in or op is w_out) for op in operands[1:]]
    seq_buf = lambda: pltpu.VMEM((ROWS, D_RWKV), F32)
    return pl.pallas_call(
        _block_kernel,
        grid=(batch // DUO, seq // step_rows),
        in_specs=in_specs,
        out_specs=pl.BlockSpec((DUO, step_rows, D_MODEL), lambda b, j: (b, j, 0)),
        out_shape=jax.ShapeDtypeStruct(x.shape, x.dtype),
        scratch_shapes=[
            pltpu.VMEM((DUO, D_SHIFT), F32),
            pltpu.VMEM((DUO, POOL_HALO, D_POOL), F32),
            pltpu.VMEM((DUO, N_HEADS // 2, HEAD, PAIR), F32),
        ] + [seq_buf() for _ in range(7)] + [
            pltpu.VMEM((max(DUO * N_CHUNKS, SUBLANES), D_RWKV), F32),
            pltpu.VMEM((D_MODEL, D_IN), BF16),
            pltpu.VMEM((len(POOL_WINDOWS), POOL_GROUP, POOL_GROUP), BF16),
            pltpu.VMEM((1, D_SHIFT), F32),
            pltpu.VMEM((2 * LORA, 2 * D_RWKV), BF16),
            pltpu.VMEM((1, D_RWKV), F32),
            pltpu.VMEM((1, D_RWKV), F32),
            pltpu.VMEM((D_MODEL, D_MODEL), BF16),
        ],
        compiler_params=pltpu.CompilerParams(
            dimension_semantics=("arbitrary", "arbitrary"),
            vmem_limit_bytes=VMEM_LIMIT_BYTES),
        name="hybrid_pool_rwkv7_block",
    )(*operands)
```
